```python
import math
import jax, jax.numpy as jnp
from jax import lax
import numpy as np

D_MODEL = 2048
BATCH = 8
SEQ = 2048
DEPTH = 2

N_MEM = 256
HEAD_DIM = 128
MIX_HEADS = 12
MIX_WIDTH = MIX_HEADS * HEAD_DIM
MEM_HEADS = 4
MEM_WIDTH = MEM_HEADS * HEAD_DIM
DIFF_QK_DIM = HEAD_DIM // 2
ROPE_THETA = 10000.0
Q_BLOCK = 128
MLSTM_CHUNK = 128
PEER_HEADS = 8
PEER_NKEYS = 128
PEER_NEXPERTS = PEER_NKEYS * PEER_NKEYS
PEER_QDIM = 256
PEER_SUBK = PEER_QDIM // 2
PEER_TOPK = 16
PEER_TOKEN_BLOCK = 128
N_MIXERS = 2
N_ATTN_LAYERS = (DEPTH + 1) // 2
N_MLSTM_LAYERS = DEPTH // 2
DN_ALPHA = (2.0 * DEPTH) ** 0.25
DN_BETA = (8.0 * DEPTH) ** -0.25
LN_EPS = 1e-5
ATTN_IN = 3 * MIX_WIDTH + MEM_WIDTH
MLSTM_IN = 4 * MIX_WIDTH + MEM_WIDTH + 4 * MIX_HEADS

kernel_name = "hybrid_diffattn_mlstm_peer_encoder"

F32 = jnp.float32


def layer_norm(x, g, b):
    xf = x.astype(F32)
    mu = xf.mean(-1, keepdims=True)
    var = jnp.square(xf - mu).mean(-1, keepdims=True)
    return ((xf - mu) * lax.rsqrt(var + LN_EPS)).astype(x.dtype) * g + b


def head_rms_norm(h, g):
    hf = h.astype(F32)
    return hf * lax.rsqrt(jnp.mean(hf * hf, -1, keepdims=True) + LN_EPS) * g.astype(F32)


def rope(t, positions):
    d = t.shape[-1]
    half = d // 2
    inv = jnp.power(ROPE_THETA, -jnp.arange(half, dtype=F32) * (2.0 / d))
    ang = positions.astype(F32)[..., None] * inv
    ang = ang.reshape(ang.shape[:2] + (1,) * (t.ndim - 3) + (half,))
    cos, sin = jnp.cos(ang), jnp.sin(ang)
    tf = t.astype(F32)
    t1, t2 = tf[..., :half], tf[..., half:]
    return jnp.concatenate([t1 * cos - t2 * sin, t2 * cos + t1 * sin], -1).astype(t.dtype)


def blocked_diff_attention(q, k, v, lam):
    B, S, H, _, dq = q.shape
    nb = S // Q_BLOCK
    scale = dq ** -0.5
    qb = jnp.moveaxis(q.reshape(B, nb, Q_BLOCK, H, 2, dq), 1, 0)

    def block(qblk):
        s = jnp.einsum('bqhmd,bkhmd->bhmqk', qblk, k).astype(F32) * scale
        p = jax.nn.softmax(s, axis=-1)
        a = p[:, :, 0] - lam * p[:, :, 1]
        return jnp.einsum('bhqk,bkhd->bqhd', a.astype(v.dtype), v)

    out = lax.map(block, qb)
    return jnp.moveaxis(out, 0, 1).reshape(B, S, H, v.shape[-1])


def diff_attention_mixer(q, k, v, positions, lam_params, norm_g, lam_init):
    B, S, _ = q.shape
    qh = rope(q.reshape(B, S, MIX_HEADS, 2, DIFF_QK_DIM), positions)
    kh = rope(k.reshape(B, S, MIX_HEADS, 2, DIFF_QK_DIM), positions)
    vh = v.reshape(B, S, MIX_HEADS, HEAD_DIM)
    lp = lam_params.astype(F32)
    lam = jnp.exp(jnp.sum(lp[0] * lp[1])) - jnp.exp(jnp.sum(lp[2] * lp[3])) + lam_init
    out = blocked_diff_attention(qh, kh, vh, lam)
    out = head_rms_norm(out, norm_g) * (1.0 - lam_init)
    return out.reshape(B, S, MIX_WIDTH).astype(q.dtype)


def mlstm_scan(q, k, v, ig, lf):
    B, H, S, d = q.shape
    L = MLSTM_CHUNK
    nc = S // L
    qc = q.reshape(B, H, nc, L, d)
    kc = k.reshape(B, H, nc, L, d)
    vc = v.reshape(B, H, nc, L, d)
    igc = ig.reshape(B, H, nc, L)
    b = jnp.cumsum(lf.reshape(B, H, nc, L), axis=-1)
    b_last = b[..., -1]
    a = b_last[..., None] - b + igc
    a_max = a.max(-1)
    wa = jnp.exp(a - a_max[..., None])
    kw = kc * wa[..., None]
    kv_chunk = jnp.einsum('bhcsk,bhcsv->bhckv', kw, vc)
    k_chunk = kw.sum(3)

    def step(carry, inp):
        C, n, m = carry
        bl, am, kvc, kcs = inp
        m_new = jnp.maximum(bl + m, am)
        decay = jnp.exp(bl + m - m_new)
        w = jnp.exp(am - m_new)
        C_new = decay[..., None, None] * C + w[..., None, None] * kvc
        n_new = decay[..., None] * n + w[..., None] * kcs
        return (C_new, n_new, m_new), (C, n, m)

    init = (jnp.zeros((B, H, d, d), F32), jnp.zeros((B, H, d), F32), jnp.zeros((B, H), F32))
    xs = (jnp.moveaxis(b_last, 2, 0), jnp.moveaxis(a_max, 2, 0),
          jnp.moveaxis(kv_chunk, 2, 0), jnp.moveaxis(k_chunk, 2, 0))
    _, (C_s, n_s, m_s) = lax.scan(step, init, xs)
    C_s = jnp.moveaxis(C_s, 0, 2)
    n_s = jnp.moveaxis(n_s, 0, 2)
    m_s = jnp.moveaxis(m_s, 0, 2)

    mask = jnp.tril(jnp.ones((L, L), dtype=bool))
    Dm = b[..., :, None] - b[..., None, :] + igc[..., None, :]
    Dm = jnp.where(mask, Dm, -jnp.inf)
    inter = b + m_s[..., None]
    m_j = jnp.maximum(inter, Dm.max(-1))
    Wd = jnp.exp(Dm - m_j[..., None])
    wi = jnp.exp(inter - m_j)
    qk = jnp.einsum('bhcjd,bhcsd->bhcjs', qc, kc) * Wd
    num = wi[..., None] * jnp.einsum('bhcjk,bhckv->bhcjv', qc, C_s) + jnp.einsum('bhcjs,bhcsv->bhcjv', qk, vc)
    den = wi * jnp.einsum('bhcjk,bhck->bhcj', qc, n_s) + qk.sum(-1)
    h = num / jnp.maximum(jnp.abs(den), jnp.exp(-m_j))[..., None]
    return h.reshape(B, H, S, d)


def mlstm_mixer(q, k, v, o, gates, gate_bias, norm_g):
    B, S, _ = q.shape

    def heads(t):
        return t.reshape(B, S, MIX_HEADS, HEAD_DIM).transpose(0, 2, 1, 3).astype(F32)

    qh, kh, vh = heads(q), heads(k) * (HEAD_DIM ** -0.5), heads(v)
    g = (gates.astype(F32).reshape(B, S, 4, MIX_HEADS) + gate_bias.astype(F32)).transpose(0, 2, 3, 1)
    ig_f, lf_f = g[:, 0], jax.nn.log_sigmoid(g[:, 1])
    ig_b, lf_b = g[:, 2], jax.nn.log_sigmoid(g[:, 3])
    h_f = mlstm_scan(qh, kh, vh, ig_f, lf_f)
    rev = lambda t: jnp.flip(t, axis=2)
    h_b = rev(mlstm_scan(rev(qh), rev(kh), rev(vh), rev(ig_b), rev(lf_b)))
    hs = head_rms_norm((h_f + h_b).transpose(0, 2, 1, 3), norm_g)
    out = jax.nn.sigmoid(o.astype(F32)).reshape(B, S, MIX_HEADS, HEAD_DIM) * hs
    return out.reshape(B, S, MIX_WIDTH).astype(q.dtype)


def memory_attention(qm, mem_k, mem_v):
    s = jnp.einsum('bshd,bmhd->bhsm', qm, mem_k).astype(F32) * (HEAD_DIM ** -0.5)
    p = jax.nn.softmax(s, axis=-1)
    return jnp.einsum('bhsm,bmhd->bshd', p.astype(mem_v.dtype), mem_v)


def peer_ffn(x, w_query, sub_keys, u_tab, v_tab):
    B, S, D = x.shape
    T = B * S
    xt = x.reshape(T, D)
    qry = (xt @ w_query).reshape(T, PEER_HEADS, 2, PEER_SUBK)
    s = jnp.einsum('thps,hpns->thpn', qry, sub_keys).astype(F32)
    sv, si = lax.top_k(s, PEER_TOPK)
    cand = (sv[:, :, 0, :, None] + sv[:, :, 1, None, :]).reshape(T, PEER_HEADS, PEER_TOPK * PEER_TOPK)
    cidx = (si[:, :, 0, :, None] * PEER_NKEYS + si[:, :, 1, None, :]).reshape(T, PEER_HEADS, PEER_TOPK * PEER_TOPK)
    top_s, pos = lax.top_k(cand, PEER_TOPK)
    eidx = jnp.take_along_axis(cidx, pos, axis=-1)
    gate = jax.nn.softmax(top_s, axis=-1)
    nb = T // PEER_TOKEN_BLOCK

    def block(args):
        xb, eb, gb = args
        act = jax.nn.gelu(jnp.einsum('thkd,td->thk', u_tab[eb], xb).astype(F32), approximate=False)
        w = (gb * act).astype(xb.dtype)
        return jnp.einsum('thk,thkd->td', w, v_tab[eb])

    out = lax.map(block, (xt.reshape(nb, PEER_TOKEN_BLOCK, D),
                          eidx.reshape(nb, PEER_TOKEN_BLOCK, PEER_HEADS, PEER_TOPK),
                          gate.reshape(nb, PEER_TOKEN_BLOCK, PEER_HEADS, PEER_TOPK)))
    return out.reshape(B, S, D)


def setup_inputs(seed: int = 0) -> dict:
    key = jax.random.key(seed)
    ks = jax.random.split(key, 24)
    D = D_MODEL
    nrm = lambda k, shape, sc: jax.random.normal(k, shape, F32) * sc
    x = nrm(ks[0], (BATCH, SEQ, D), 1.0)
    mem = nrm(ks[1], (BATCH, N_MEM, D), 1.0)
    positions = jnp.broadcast_to(jnp.arange(SEQ, dtype=jnp.int32)[None, :], (BATCH, SEQ))
    attn_w_in = nrm(ks[2], (N_ATTN_LAYERS, D, ATTN_IN), D ** -0.5)
    attn_lambda = nrm(ks[3], (N_ATTN_LAYERS, 4, DIFF_QK_DIM), 0.1)
    attn_head_norm = 1.0 + nrm(ks[4], (N_ATTN_LAYERS, MIX_HEADS, HEAD_DIM), 0.02)
    mlstm_w_in = nrm(ks[5], (N_MLSTM_LAYERS, D, MLSTM_IN), D ** -0.5)
    noise = nrm(ks[6], (N_MLSTM_LAYERS, 4, MIX_HEADS), 0.1)
    f_base = jnp.linspace(3.0, 6.0, MIX_HEADS, dtype=F32)
    base = jnp.stack([jnp.zeros_like(f_base), f_base, jnp.zeros_like(f_base), f_base])[None]
    mlstm_gate_bias = base + noise
    mlstm_head_norm = 1.0 + nrm(ks[7], (N_MLSTM_LAYERS, MIX_HEADS, HEAD_DIM), 0.02)
    mem_w_kv = nrm(ks[8], (D, 2 * MEM_WIDTH), D ** -0.5)
    w_out = nrm(ks[9], (DEPTH, MIX_WIDTH + MEM_WIDTH, D), DN_BETA * (MIX_WIDTH + MEM_WIDTH) ** -0.5)
    ln_mix_g = 1.0 + nrm(ks[10], (DEPTH, D), 0.02)
    ln_mix_b = nrm(ks[11], (DEPTH, D), 0.02)
    peer_w_query = nrm(ks[12], (DEPTH, D, PEER_HEADS * PEER_QDIM), D ** -0.5)
    peer_sub_keys = nrm(ks[13], (DEPTH, PEER_HEADS, 2, PEER_NKEYS, PEER_SUBK), PEER_SUBK ** -0.5)
    peer_u = nrm(ks[14], (DEPTH, PEER_NEXPERTS, D), D ** -0.5)
    peer_v = nrm(ks[15], (DEPTH, PEER_NEXPERTS, D), DN_BETA * PEER_HEADS ** -0.5)
    ln_ffn_g = 1.0 + nrm(ks[16], (DEPTH, D), 0.02)
    ln_ffn_b = nrm(ks[17], (DEPTH, D), 0.02)
    return {"x": x, "mem": mem, "positions": positions,
            "attn_w_in": attn_w_in, "attn_lambda": attn_lambda, "attn_head_norm": attn_head_norm,
            "mlstm_w_in": mlstm_w_in, "mlstm_gate_bias": mlstm_gate_bias, "mlstm_head_norm": mlstm_head_norm,
            "mem_w_kv": mem_w_kv, "w_out": w_out, "ln_mix_g": ln_mix_g, "ln_mix_b": ln_mix_b,
            "peer_w_query": peer_w_query, "peer_sub_keys": peer_sub_keys, "peer_u": peer_u, "peer_v": peer_v,
            "ln_ffn_g": ln_ffn_g, "ln_ffn_b": ln_ffn_b}


def reference(x, mem, positions, attn_w_in, attn_lambda, attn_head_norm, mlstm_w_in, mlstm_gate_bias,
              mlstm_head_norm, mem_w_kv, w_out, ln_mix_g, ln_mix_b, peer_w_query, peer_sub_keys, peer_u, peer_v,
              ln_ffn_g, ln_ffn_b):
    B, S, D = x.shape
    M = mem.shape[1]
    mem_kv = (mem @ mem_w_kv).reshape(B, M, 2, MEM_HEADS, HEAD_DIM)
    mem_k, mem_v = mem_kv[:, :, 0], mem_kv[:, :, 1]
    h = x
    for i in range(DEPTH):
        j = i // N_MIXERS
        if i % N_MIXERS == 0:
            proj = h @ attn_w_in[j]
            q, k, v, qm = jnp.split(proj, [MIX_WIDTH, 2 * MIX_WIDTH, 3 * MIX_WIDTH], axis=-1)
            lam_init = 0.8 - 0.6 * math.exp(-0.3 * i)
            mix = diff_attention_mixer(q, k, v, positions, attn_lambda[j], attn_head_norm[j], lam_init)
        else:
            proj = h @ mlstm_w_in[j]
            q, k, v, o, qm, gates = jnp.split(
                proj, [MIX_WIDTH, 2 * MIX_WIDTH, 3 * MIX_WIDTH, 4 * MIX_WIDTH, 4 * MIX_WIDTH + MEM_WIDTH], axis=-1)
            mix = mlstm_mixer(q, k, v, o, gates, mlstm_gate_bias[j], mlstm_head_norm[j])
        mem_out = memory_attention(qm.reshape(B, S, MEM_HEADS, HEAD_DIM), mem_k, mem_v).reshape(B, S, MEM_WIDTH)
        y = jnp.concatenate([mix, mem_out], axis=-1) @ w_out[i]
        h = layer_norm(DN_ALPHA * h + y, ln_mix_g[i], ln_mix_b[i])
        f = peer_ffn(h, peer_w_query[i], peer_sub_keys[i], peer_u[i], peer_v[i])
        h = layer_norm(DN_ALPHA * h + f, ln_ffn_g[i], ln_ffn_b[i])
    return h
```

```python
import functools
import math

import jax
import jax.numpy as jnp
from jax import lax
from jax.experimental import pallas as pl
from jax.experimental.pallas import tpu as pltpu

F32 = jnp.float32
BF16 = jnp.bfloat16

HEAD_DIM = 128
MIX_HEADS = 12
MIX_WIDTH = MIX_HEADS * HEAD_DIM
MEM_HEADS = 4
MEM_WIDTH = MEM_HEADS * HEAD_DIM
DIFF_QK_DIM = HEAD_DIM // 2
ROPE_THETA = 10000.0
MLSTM_CHUNK = 128
PEER_HEADS = 8
PEER_NKEYS = 128
PEER_TOPK = 16
PEER_SUBK = 128
DEPTH = 2
DN_ALPHA = (2.0 * DEPTH) ** 0.25
LN_EPS = 1e-5

LANES = 128
SUBLANES = 8
VMEM_LIMIT_BYTES = 48 * 1024 * 1024
NEG_INF = float("-inf")


def _params(*sem):
    return pltpu.CompilerParams(dimension_semantics=sem, vmem_limit_bytes=VMEM_LIMIT_BYTES)


def _nt_dot(a, b):
    return lax.dot_general(a, b, (((1,), (1,)), ((), ())), preferred_element_type=F32)


def _matmul_kernel(a_ref, b_ref, o_ref):
    o_ref[...] = jnp.dot(a_ref[...], b_ref[...], preferred_element_type=F32).astype(o_ref.dtype)


def _matmul(a, b, out_dtype, tm, tn):
    m, k = a.shape
    n = b.shape[1]
    tm, tn = min(tm, m), min(tn, n)
    return pl.pallas_call(
        _matmul_kernel,
        grid=(m // tm, n // tn),
        in_specs=[pl.BlockSpec((tm, k), lambda i, j: (i, 0)),
                  pl.BlockSpec((k, tn), lambda i, j: (0, j))],
        out_specs=pl.BlockSpec((tm, tn), lambda i, j: (i, j)),
        out_shape=jax.ShapeDtypeStruct((m, n), out_dtype),
        compiler_params=_params("parallel", "arbitrary"),
        name="matmul",
    )(a, b)


def _rope_kernel(x_ref, cos_ref, sin_ref, o_ref):
    x = x_ref[...].astype(F32)
    lane = lax.broadcasted_iota(jnp.int32, x.shape, 1)
    first = (lane % DIFF_QK_DIM) < (DIFF_QK_DIM // 2)
    rot = jnp.where(first, pltpu.roll(x, LANES - DIFF_QK_DIM // 2, 1), pltpu.roll(x, DIFF_QK_DIM // 2, 1))
    y = x * cos_ref[...] + rot * sin_ref[...]
    scale = jnp.where(pl.program_id(1) < MIX_HEADS, DIFF_QK_DIM ** -0.5, 1.0).astype(F32)
    o_ref[...] = (y * scale).astype(o_ref.dtype)


def _rope(proj, cos_t, sin_t, tm):
    t = proj.shape[0]
    tm = min(tm, t)
    blk = pl.BlockSpec((tm, LANES), lambda i, j: (i, j))
    tab = pl.BlockSpec((tm, LANES), lambda i, j: (i, 0))
    return pl.pallas_call(
        _rope_kernel,
        grid=(t // tm, 2 * MIX_HEADS),
        in_specs=[blk, tab, tab],
        out_specs=blk,
        out_shape=jax.ShapeDtypeStruct((t, 2 * MIX_WIDTH), BF16),
        compiler_params=_params("parallel", "arbitrary"),
        name="rope",
    )(proj, cos_t, sin_t)


def _diff_attn_kernel(lam_ref, q_ref, k_ref, v_ref, g_ref, o_ref, *, lam_init):
    tq = q_ref.shape[0]
    q = q_ref[...]
    lane = lax.broadcasted_iota(jnp.int32, q.shape, 1)
    zero = jnp.zeros_like(q)
    qq = jnp.concatenate([jnp.where(lane < DIFF_QK_DIM, q, zero), jnp.where(lane >= DIFF_QK_DIM, q, zero)], axis=0)
    s = _nt_dot(qq, k_ref[...])
    e = jnp.exp(s - jnp.max(s, axis=-1, keepdims=True))
    l = jnp.sum(e, axis=-1, keepdims=True)
    pv = jnp.dot(e.astype(BF16), v_ref[...], preferred_element_type=F32) / l
    lp = lam_ref[...]
    lam = (jnp.exp(jnp.sum(lp[0:1] * lp[1:2], axis=-1, keepdims=True))
           - jnp.exp(jnp.sum(lp[2:3] * lp[3:4], axis=-1, keepdims=True)) + lam_init)
    out = pv[:tq] - lam * pv[tq:]
    out = out * lax.rsqrt(jnp.mean(out * out, axis=-1, keepdims=True) + LN_EPS) * g_ref[...] * (1.0 - lam_init)
    o_ref[...] = out.astype(o_ref.dtype)


def _diff_attention(qk, proj, lam_params, norm_g, batch, seq, lam_init, tq):
    tq = min(tq, seq)
    nq = seq // tq
    return pl.pallas_call(
        functools.partial(_diff_attn_kernel, lam_init=lam_init),
        grid=(batch, MIX_HEADS, nq),
        in_specs=[pl.BlockSpec((4, DIFF_QK_DIM), lambda b, h, i: (0, 0)),
                  pl.BlockSpec((tq, HEAD_DIM), lambda b, h, i: (b * nq + i, h)),
                  pl.BlockSpec((seq, HEAD_DIM), lambda b, h, i: (b, MIX_HEADS + h)),
                  pl.BlockSpec((seq, HEAD_DIM), lambda b, h, i: (b, 2 * MIX_HEADS + h)),
                  pl.BlockSpec((None, 1, HEAD_DIM), lambda b, h, i: (h, 0, 0))],
        out_specs=pl.BlockSpec((tq, HEAD_DIM), lambda b, h, i: (b * nq + i, h)),
        out_shape=jax.ShapeDtypeStruct((batch * seq, MIX_WIDTH), BF16),
        compiler_params=_params("parallel", "arbitrary", "arbitrary"),
        name="diff_attention",
    )(lam_params, qk, qk, proj, norm_g.reshape(MIX_HEADS, 1, HEAD_DIM))


def _mem_attn_kernel(q_ref, k_ref, v_ref, o_ref):
    s = _nt_dot(q_ref[...], k_ref[...]) * (HEAD_DIM ** -0.5)
    e = jnp.exp(s - jnp.max(s, axis=-1, keepdims=True))
    l = jnp.sum(e, axis=-1, keepdims=True)
    o_ref[...] = (jnp.dot(e.astype(BF16), v_ref[...], preferred_element_type=F32) / l).astype(o_ref.dtype)


def _mem_attention(proj, qm_block0, mem_kv, batch, seq, n_mem, tq):
    tq = min(tq, seq)
    nq = seq // tq
    return pl.pallas_call(
        _mem_attn_kernel,
        grid=(batch, MEM_HEADS, nq),
        in_specs=[pl.BlockSpec((tq, HEAD_DIM), lambda b, h, i: (b * nq + i, qm_block0 + h)),
                  pl.BlockSpec((n_mem, HEAD_DIM), lambda b, h, i: (b, h)),
                  pl.BlockSpec((n_mem, HEAD_DIM), lambda b, h, i: (b, MEM_HEADS + h))],
        out_specs=pl.BlockSpec((tq, HEAD_DIM), lambda b, h, i: (b * nq + i, h)),
        out_shape=jax.ShapeDtypeStruct((batch * seq, MEM_WIDTH), BF16),
        compiler_params=_params("parallel", "arbitrary", "arbitrary"),
        name="mem_attention",
    )(proj, mem_kv, mem_kv)


def _layer_norm_rows(z, g, b):
    mu = jnp.mean(z, axis=-1, keepdims=True)
    zc = z - mu
    var = jnp.mean(zc * zc, axis=-1, keepdims=True)
    return zc * lax.rsqrt(var + LN_EPS) * g + b


def _outproj_ln_kernel(mix_ref, mem_ref, h_ref, w_mix_ref, w_mem_ref, g_ref, b_ref, o_ref, obf_ref):
    y = (jnp.dot(mix_ref[...], w_mix_ref[...], preferred_element_type=F32)
         + jnp.dot(mem_ref[...], w_mem_ref[...], preferred_element_type=F32))
    out = _layer_norm_rows(DN_ALPHA * h_ref[...] + y, g_ref[...], b_ref[...])
    o_ref[...] = out
    obf_ref[...] = out.astype(BF16)


def _outproj_ln(mix, mem_out, h, w_out, g, b, tm):
    t, d = h.shape
    tm = min(tm, t)
    row = lambda i: (i, 0)
    fixed = lambda i: (0, 0)
    return pl.pallas_call(
        _outproj_ln_kernel,
        grid=(t // tm,),
        in_specs=[pl.BlockSpec((tm, MIX_WIDTH), row), pl.BlockSpec((tm, MEM_WIDTH), row),
                  pl.BlockSpec((tm, d), row),
                  pl.BlockSpec((MIX_WIDTH, d), fixed), pl.BlockSpec((MEM_WIDTH, d), fixed),
                  pl.BlockSpec((1, d), fixed), pl.BlockSpec((1, d), fixed)],
        out_specs=[pl.BlockSpec((tm, d), row), pl.BlockSpec((tm, d), row)],
        out_shape=[jax.ShapeDtypeStruct((t, d), F32), jax.ShapeDtypeStruct((t, d), BF16)],
        compiler_params=_params("parallel"),
        name="outproj_ln",
    )(mix, mem_out, h, w_out[:MIX_WIDTH].astype(BF16), w_out[MIX_WIDTH:].astype(BF16),
      g.reshape(1, d), b.reshape(1, d))


def _residual_ln_kernel(h_ref, f_ref, g_ref, b_ref, o_ref, obf_ref):
    out = _layer_norm_rows(DN_ALPHA * h_ref[...] + f_ref[...], g_ref[...], b_ref[...])
    o_ref[...] = out
    obf_ref[...] = out.astype(BF16)


def _residual_ln(h, f, g, b, tm):
    t, d = h.shape
    tm = min(tm, t)
    row = lambda i: (i, 0)
    fixed = lambda i: (0, 0)
    return pl.pallas_call(
        _residual_ln_kernel,
        grid=(t // tm,),
        in_specs=[pl.BlockSpec((tm, d), row), pl.BlockSpec((tm, d), row),
                  pl.BlockSpec((1, d), fixed), pl.BlockSpec((1, d), fixed)],
        out_specs=[pl.BlockSpec((tm, d), row), pl.BlockSpec((tm, d), row)],
        out_shape=[jax.ShapeDtypeStruct((t, d), F32), jax.ShapeDtypeStruct((t, d), BF16)],
        compiler_params=_params("parallel"),
        name="residual_ln",
    )(h, f, g.reshape(1, d), b.reshape(1, d))


def _lane_cumsum(x, reverse):
    lane = lax.broadcasted_iota(jnp.int32, x.shape, 1)
    sh = 1
    while sh < LANES:
        if reverse:
            x = x + jnp.where(lane < LANES - sh, pltpu.roll(x, LANES - sh, 1), 0.0)
        else:
            x = x + jnp.where(lane >= sh, pltpu.roll(x, sh, 1), 0.0)
        sh *= 2
    return x


def _log_sigmoid(x):
    return jnp.minimum(x, 0.0) - jnp.log1p(jnp.exp(-jnp.abs(x)))


def _mlstm_chunk(c, q_ref, k_ref, v_ref, ig_row, fg_row, c_ref, n_ref, m_ref, h_ref, reverse):
    L = MLSTM_CHUNK
    scale = HEAD_DIM ** -0.5
    r0 = pl.multiple_of(c * L, L)
    q = q_ref[pl.ds(r0, L), :]
    k = k_ref[pl.ds(r0, L), :]
    v = v_ref[pl.ds(r0, L), :]
    lf = _log_sigmoid(fg_row)
    b_row8 = _lane_cumsum(lf, reverse)
    b_last = jnp.sum(lf[0:1], axis=-1, keepdims=True)
    b_row = jnp.broadcast_to(b_row8[0:1], (L, L))
    ig_rowb = jnp.broadcast_to(ig_row[0:1], (L, L))
    b_col = b_row.T
    ig_col = ig_rowb.T
    jj = lax.broadcasted_iota(jnp.int32, (L, L), 0)
    ss = lax.broadcasted_iota(jnp.int32, (L, L), 1)
    mask = (ss >= jj) if reverse else (ss <= jj)
    dm = jnp.where(mask, b_col - b_row + ig_rowb, NEG_INF)
    m_old = m_ref[...]
    inter = b_col[:, 0:1] + m_old
    m_j = jnp.maximum(inter, jnp.max(dm, axis=-1, keepdims=True))
    wd = jnp.exp(dm - m_j)
    wi = jnp.exp(inter - m_j)
    qk = _nt_dot(q, k) * scale * wd
    c_old = c_ref[...]
    n_old = n_ref[...]
    num = (wi * jnp.dot(q, c_old.astype(BF16), preferred_element_type=F32)
           + jnp.dot(qk.astype(BF16), v, preferred_element_type=F32))
    den = wi * jnp.sum(q.astype(F32) * n_old, axis=-1, keepdims=True) + jnp.sum(qk, axis=-1, keepdims=True)
    h_ref[pl.ds(r0, L), :] = num / jnp.maximum(jnp.abs(den), jnp.exp(-m_j))
    a_col = b_last - b_col[:, 0:1] + ig_col[:, 0:1]
    a_max = jnp.max(a_col, axis=0, keepdims=True)
    wa = jnp.exp(a_col - a_max)
    kf = k.astype(F32)
    vw = (v.astype(F32) * wa).astype(BF16)
    kv = jnp.dot(kf.T.astype(BF16), vw, preferred_element_type=F32) * scale
    kc = jnp.sum(kf * wa, axis=0, keepdims=True) * scale
    m_new = jnp.maximum(b_last + m_old, a_max)
    decay = jnp.exp(b_last + m_old - m_new)
    w = jnp.exp(a_max - m_new)
    c_ref[...] = decay * c_old + w * kv
    n_ref[...] = decay * n_old + w * kc
    m_ref[...] = m_new


def _mlstm_kernel(bias_ref, q_ref, k_ref, v_ref, o_ref, g_ref, ng_ref, out_ref,
                  hf_ref, hb_ref, cf_ref, cb_ref, nf_ref, nb_ref, mf_ref, mb_ref):
    hd = pl.program_id(1)
    nc = q_ref.shape[0] // MLSTM_CHUNK
    for r in (cf_ref, cb_ref, nf_ref, nb_ref, mf_ref, mb_ref):
        r[...] = jnp.zeros_like(r)

    def gate_row(kind, c):
        row = g_ref[kind, pl.ds(c, 1), :] + bias_ref[kind, hd]
        return jnp.broadcast_to(row, (SUBLANES, MLSTM_CHUNK))

    def body(c, carry):
        _mlstm_chunk(c, q_ref, k_ref, v_ref, gate_row(0, c), gate_row(1, c),
                     cf_ref, nf_ref, mf_ref, hf_ref, False)
        cr = nc - 1 - c
        _mlstm_chunk(cr, q_ref, k_ref, v_ref, gate_row(2, cr), gate_row(3, cr),
                     cb_ref, nb_ref, mb_ref, hb_ref, True)
        return carry

    lax.fori_loop(0, nc, body, 0)
    hs = hf_ref[...] + hb_ref[...]
    hs = hs * lax.rsqrt(jnp.mean(hs * hs, axis=-1, keepdims=True) + LN_EPS) * ng_ref[...]
    out_ref[...] = (jax.nn.sigmoid(o_ref[...].astype(F32)) * hs).astype(out_ref.dtype)


def _mlstm(proj, gates_t, gate_bias, norm_g, batch, seq):
    nc = seq // MLSTM_CHUNK
    col = lambda off: pl.BlockSpec((seq, HEAD_DIM), lambda b, h: (b, off + h))
    st = lambda shape: pltpu.VMEM(shape, F32)
    return pl.pallas_call(
        _mlstm_kernel,
        grid=(batch, MIX_HEADS),
        in_specs=[pl.BlockSpec(memory_space=pltpu.SMEM),
                  col(0), col(MIX_HEADS), col(2 * MIX_HEADS), col(3 * MIX_HEADS),
                  pl.BlockSpec((None, 4, None, nc, MLSTM_CHUNK), lambda b, h: (b, 0, h, 0, 0)),
                  pl.BlockSpec((None, 1, HEAD_DIM), lambda b, h: (h, 0, 0))],
        out_specs=pl.BlockSpec((seq, HEAD_DIM), lambda b, h: (b, h)),
        out_shape=jax.ShapeDtypeStruct((batch * seq, MIX_WIDTH), BF16),
        scratch_shapes=[st((seq, HEAD_DIM)), st((seq, HEAD_DIM)),
                        st((HEAD_DIM, HEAD_DIM)), st((HEAD_DIM, HEAD_DIM)),
                        st((1, HEAD_DIM)), st((1, HEAD_DIM)), st((1, 1)), st((1, 1))],
        compiler_params=_params("parallel", "arbitrary"),
        name="mlstm",
    )(gate_bias, proj, proj, proj, proj, gates_t, norm_g.reshape(MIX_HEADS, 1, HEAD_DIM))


def _topk_rows(vals, n_rows, payload=None):
    row = lax.broadcasted_iota(jnp.int32, vals.shape, 0)
    top_v, top_i = [], []
    for _ in range(PEER_TOPK):
        mx = jnp.max(vals, axis=0, keepdims=True)
        pos = jnp.min(jnp.where(vals == mx, row, n_rows), axis=0, keepdims=True)
        hit = row == pos
        top_v.append(mx)
        top_i.append(pos if payload is None else jnp.max(jnp.where(hit, payload, -1), axis=0, keepdims=True))
        vals = jnp.where(hit, NEG_INF, vals)
    return jnp.concatenate(top_v, axis=0), jnp.concatenate(top_i, axis=0)


def _peer_subkey_topk_kernel(q_ref, sk_ref, sv_ref, si_ref):
    s_t = _nt_dot(sk_ref[...], q_ref[...])
    sv, si = _topk_rows(s_t, PEER_NKEYS)
    sv_ref[...] = sv
    si_ref[...] = si


def _peer_expert_topk_kernel(sv0_ref, si0_ref, sv1_ref, si1_ref, idx_ref, gate_ref):
    sv0, si0, sv1, si1 = sv0_ref[...], si0_ref[...], sv1_ref[...], si1_ref[...]
    cand = jnp.concatenate([sv0[i:i + 1] + sv1 for i in range(PEER_TOPK)], axis=0)
    cidx = jnp.concatenate([si0[i:i + 1] * PEER_NKEYS + si1 for i in range(PEER_TOPK)], axis=0)
    top_s, eidx = _topk_rows(cand, PEER_TOPK * PEER_TOPK, payload=cidx)
    e = jnp.exp(top_s - jnp.max(top_s, axis=0, keepdims=True))
    idx_ref[...] = eidx
    gate_ref[...] = e / jnp.sum(e, axis=0, keepdims=True)


def _peer_route(qry, sub_keys, tm):
    t = qry.shape[0]
    tm = min(tm, t)
    nhp = 2 * PEER_HEADS
    half = pl.BlockSpec((None, PEER_TOPK, tm), lambda i, j: (j, 0, i))
    sv, si = pl.pallas_call(
        _peer_subkey_topk_kernel,
        grid=(t // tm, nhp),
        in_specs=[pl.BlockSpec((tm, PEER_SUBK), lambda i, j: (i, j)),
                  pl.BlockSpec((None, PEER_NKEYS, PEER_SUBK), lambda i, j: (j, 0, 0))],
        out_specs=[half, half],
        out_shape=[jax.ShapeDtypeStruct((nhp, PEER_TOPK, t), F32),
                   jax.ShapeDtypeStruct((nhp, PEER_TOPK, t), jnp.int32)],
        compiler_params=_params("parallel", "arbitrary"),
        name="peer_subkey_topk",
    )(qry, sub_keys.reshape(nhp, PEER_NKEYS, PEER_SUBK).astype(BF16))
    first = pl.BlockSpec((None, PEER_TOPK, tm), lambda i, j: (2 * j, 0, i))
    second = pl.BlockSpec((None, PEER_TOPK, tm), lambda i, j: (2 * j + 1, 0, i))
    head = pl.BlockSpec((None, PEER_TOPK, tm), lambda i, j: (j, 0, i))
    eidx, gate = pl.pallas_call(
        _peer_expert_topk_kernel,
        grid=(t // tm, PEER_HEADS),
        in_specs=[first, first, second, second],
        out_specs=[head, head],
        out_shape=[jax.ShapeDtypeStruct((PEER_HEADS, PEER_TOPK, t), jnp.int32),
                   jax.ShapeDtypeStruct((PEER_HEADS, PEER_TOPK, t), F32)],
        compiler_params=_params("parallel", "arbitrary"),
        name="peer_expert_topk",
    )(sv, si, sv, si)
    n_sel = PEER_HEADS * PEER_TOPK
    return eidx.reshape(n_sel, t).T, gate.reshape(n_sel, t).T


N_SEL = PEER_HEADS * PEER_TOPK
GATHER_TOKENS = 8


def _gelu(x):
    return 0.5 * x * (1.0 + lax.erf(x * (2.0 ** -0.5)))


def _peer_gather_kernel(idx_ref, idx_next_ref, x_ref, gate_ref, uv_hbm, f_ref, buf, sem, p_ref, w_ref):
    tb = x_ref.shape[0]
    rows = x_ref.shape[1]
    i = pl.program_id(0)
    slot = i % 2

    def issue(ids_ref, dst_slot):
        def per_token(t, carry):
            for k in range(N_SEL):
                pltpu.make_async_copy(uv_hbm.at[ids_ref[t, k]], buf.at[dst_slot, t * N_SEL + k],
                                      sem.at[dst_slot]).start(priority=k % 2)
            return carry
        lax.fori_loop(0, tb, per_token, 0)

    @pl.when(i == 0)
    def _():
        issue(idx_ref, 0)

    @pl.when(i + 1 < pl.num_programs(0))
    def _():
        issue(idx_next_ref, 1 - slot)

    pltpu.make_async_copy(uv_hbm.at[pl.ds(0, tb * N_SEL)], buf.at[slot], sem.at[slot]).wait()

    def per_token(t, carry):
        xs = x_ref[t].astype(F32)
        base = t * N_SEL
        for k in range(N_SEL):
            p = buf[slot, base + k, 0:rows, :].astype(F32) * xs
            part = p[0:SUBLANES]
            for r in range(1, rows // SUBLANES):
                part = part + p[r * SUBLANES:(r + 1) * SUBLANES]
            p_ref[pl.ds(k * SUBLANES, SUBLANES), :] = part
        s = p_ref[pl.ds(0, N_SEL, stride=SUBLANES), :]
        for r in range(1, SUBLANES):
            s = s + p_ref[pl.ds(r, N_SEL, stride=SUBLANES), :]
        act = jnp.sum(s, axis=-1, keepdims=True)
        gate_col = jnp.broadcast_to(gate_ref[pl.ds(t, 1), :], (N_SEL, N_SEL)).T
        w_ref[...] = gate_col * _gelu(act)
        acc = jnp.zeros((rows, LANES), F32)
        for k in range(N_SEL):
            acc = acc + w_ref[pl.ds(k, 1), :] * buf[slot, base + k, rows:2 * rows, :].astype(F32)
        f_ref[t] = acc
        return carry

    lax.fori_loop(0, tb, per_token, 0)


def _peer_experts(x_bf, eidx, gate, uv):
    t, d = x_bf.shape
    rows = d // LANES
    tb = min(GATHER_TOKENS, t)
    xs = x_bf.reshape(t, rows, LANES)
    nblk = t // tb
    f = pl.pallas_call(
        _peer_gather_kernel,
        grid=(nblk,),
        in_specs=[pl.BlockSpec((tb, N_SEL), lambda i: (i, 0), memory_space=pltpu.SMEM),
                  pl.BlockSpec((tb, N_SEL), lambda i: (jnp.minimum(i + 1, nblk - 1), 0), memory_space=pltpu.SMEM),
                  pl.BlockSpec((tb, rows, LANES), lambda i: (i, 0, 0)),
                  pl.BlockSpec((tb, N_SEL), lambda i: (i, 0)),
                  pl.BlockSpec(memory_space=pl.ANY)],
        out_specs=pl.BlockSpec((tb, rows, LANES), lambda i: (i, 0, 0)),
        out_shape=jax.ShapeDtypeStruct((t, rows, LANES), F32),
        scratch_shapes=[pltpu.VMEM((2, tb * N_SEL, 2 * rows, LANES), BF16),
                        pltpu.SemaphoreType.DMA((2,)),
                        pltpu.VMEM((N_SEL * SUBLANES, LANES), F32),
                        pltpu.VMEM((N_SEL, LANES), F32)],
        compiler_params=_params("arbitrary"),
        name="peer_experts",
    )(eidx, eidx, xs, gate, uv)
    return f.reshape(t, d)


def _peer_ffn(h, h_bf, w_query, sub_keys, u_tab, v_tab, ln_g, ln_b):
    t, d = h.shape
    rows = d // LANES
    n_exp = u_tab.shape[0]
    qry = _matmul(h_bf, w_query.astype(BF16), BF16, 1024, 512)
    eidx, gate = _peer_route(qry, sub_keys, 256)
    uv = jnp.concatenate([u_tab.reshape(n_exp, rows, LANES), v_tab.reshape(n_exp, rows, LANES)],
                         axis=1).astype(BF16)
    f = _peer_experts(h_bf, eidx, gate, uv)
    return _residual_ln(h, f, ln_g, ln_b, 256)


def _rope_tables(positions):
    half = DIFF_QK_DIM // 2
    inv = jnp.power(ROPE_THETA, -jnp.arange(half, dtype=F32) * (2.0 / DIFF_QK_DIM))
    ang = positions.astype(F32).reshape(-1, 1) * inv
    cos, sin = jnp.cos(ang), jnp.sin(ang)
    return (jnp.concatenate([cos, cos, cos, cos], axis=-1),
            jnp.concatenate([-sin, sin, -sin, sin], axis=-1))


def kernel(x, mem, positions, attn_w_in, attn_lambda, attn_head_norm, mlstm_w_in, mlstm_gate_bias, mlstm_head_norm, mem_w_kv, w_out, ln_mix_g, ln_mix_b, peer_w_query, peer_sub_keys, peer_u, peer_v, ln_ffn_g, ln_ffn_b):
    batch, seq, d = x.shape
    n_mem = mem.shape[1]
    t = batch * seq
    mem_kv = _matmul(mem.reshape(batch * n_mem, d).astype(BF16), mem_w_kv.astype(BF16), BF16, 1024, 512)
    cos_t, sin_t = _rope_tables(positions)
    h = x.reshape(t, d)
    h_bf = h.astype(BF16)
    for i in range(DEPTH):
        j = i // 2
        if i % 2 == 0:
            proj = _matmul(h_bf, attn_w_in[j].astype(BF16), BF16, 1024, 512)
            qk = _rope(proj, cos_t, sin_t, 1024)
            lam_init = 0.8 - 0.6 * math.exp(-0.3 * i)
            mix = _diff_attention(qk, proj, attn_lambda[j], attn_head_norm[j], batch, seq, lam_init, 256)
            qm_block0 = 3 * MIX_HEADS
        else:
            n_main = 4 * MIX_WIDTH + MEM_WIDTH
            w_in = mlstm_w_in[j]
            proj = _matmul(h_bf, w_in[:, :n_main].astype(BF16), BF16, 1024, 512)
            n_gates = 4 * MIX_HEADS
            w_gates = jnp.pad(w_in[:, n_main:], ((0, 0), (0, LANES - n_gates))).astype(BF16)
            gates = _matmul(h_bf, w_gates, F32, 1024, LANES)[:, :n_gates]
            gates_t = gates.reshape(batch, seq // MLSTM_CHUNK, MLSTM_CHUNK, 4, MIX_HEADS).transpose(0, 3, 4, 1, 2)
            mix = _mlstm(proj, gates_t, mlstm_gate_bias[j], mlstm_head_norm[j], batch, seq)
            qm_block0 = 4 * MIX_HEADS
        mem_out = _mem_attention(proj, qm_block0, mem_kv, batch, seq, n_mem, 512)
        h, h_bf = _outproj_ln(mix, mem_out, h, w_out[i], ln_mix_g[i], ln_mix_b[i], 256)
        h, h_bf = _peer_ffn(h, h_bf, peer_w_query[i], peer_sub_keys[i], peer_u[i], peer_v[i],
                            ln_ffn_g[i], ln_ffn_b[i])
    return h.reshape(batch, seq, d)
```

```python
import functools
import math

import jax
import jax.numpy as jnp
from jax import lax
from jax.experimental import pallas as pl
from jax.experimental.pallas import tpu as pltpu

F32 = jnp.float32
BF16 = jnp.bfloat16

HEAD_DIM = 128
MIX_HEADS = 12
MIX_WIDTH = MIX_HEADS * HEAD_DIM
MEM_HEADS = 4
MEM_WIDTH = MEM_HEADS * HEAD_DIM
DIFF_QK_DIM = HEAD_DIM // 2
ROPE_THETA = 10000.0
MLSTM_CHUNK = 128
PEER_HEADS = 8
PEER_NKEYS = 128
PEER_TOPK = 16
PEER_SUBK = 128
DEPTH = 2
DN_ALPHA = (2.0 * DEPTH) ** 0.25
LN_EPS = 1e-5

LANES = 128
SUBLANES = 8
VMEM_LIMIT_BYTES = 48 * 1024 * 1024
NEG_INF = float("-inf")
LOG2_E = math.log2(math.e)
ATTN_SUB_ROWS = 128
MLSTM_UNROLL = 2


def _params(*sem):
    return pltpu.CompilerParams(dimension_semantics=sem, vmem_limit_bytes=VMEM_LIMIT_BYTES)


def _nt_dot(a, b):
    return lax.dot_general(a, b, (((1,), (1,)), ((), ())), preferred_element_type=F32)


def _matmul_kernel(a_ref, b_ref, o_ref):
    o_ref[...] = jnp.dot(a_ref[...], b_ref[...], preferred_element_type=F32).astype(o_ref.dtype)


def _matmul(a, b, out_dtype, tm, tn):
    m, k = a.shape
    n = b.shape[1]
    tm, tn = min(tm, m), min(tn, n)
    return pl.pallas_call(
        _matmul_kernel,
        grid=(m // tm, n // tn),
        in_specs=[pl.BlockSpec((tm, k), lambda i, j: (i, 0)),
                  pl.BlockSpec((k, tn), lambda i, j: (0, j))],
        out_specs=pl.BlockSpec((tm, tn), lambda i, j: (i, j)),
        out_shape=jax.ShapeDtypeStruct((m, n), out_dtype),
        compiler_params=_params("parallel", "arbitrary"),
        name="matmul",
    )(a, b)


def _rope_kernel(x_ref, cos_ref, sin_ref, o_ref):
    x = x_ref[...].astype(F32)
    lane = lax.broadcasted_iota(jnp.int32, x.shape, 1)
    first = (lane % DIFF_QK_DIM) < (DIFF_QK_DIM // 2)
    rot = jnp.where(first, pltpu.roll(x, LANES - DIFF_QK_DIM // 2, 1), pltpu.roll(x, DIFF_QK_DIM // 2, 1))
    y = x * cos_ref[...] + rot * sin_ref[...]
    scale = jnp.where(pl.program_id(1) < MIX_HEADS, DIFF_QK_DIM ** -0.5 * LOG2_E, 1.0).astype(F32)
    o_ref[...] = (y * scale).astype(o_ref.dtype)


def _rope(proj, cos_t, sin_t, tm):
    t = proj.shape[0]
    tm = min(tm, t)
    blk = pl.BlockSpec((tm, LANES), lambda i, j: (i, j))
    tab = pl.BlockSpec((tm, LANES), lambda i, j: (i, 0))
    return pl.pallas_call(
        _rope_kernel,
        grid=(t // tm, 2 * MIX_HEADS),
        in_specs=[blk, tab, tab],
        out_specs=blk,
        out_shape=jax.ShapeDtypeStruct((t, 2 * MIX_WIDTH), BF16),
        compiler_params=_params("parallel", "arbitrary"),
        name="rope",
    )(proj, cos_t, sin_t)


def _diff_attn_kernel(lam_ref, q_ref, k_ref, v_ref, g_ref, o_ref, vext_ref, *, lam_init):
    @pl.when(pl.program_id(2) == 0)
    def _():
        vext_ref[:, 0:HEAD_DIM] = v_ref[...]
        vext_ref[:, HEAD_DIM:2 * HEAD_DIM] = jnp.ones(v_ref.shape, v_ref.dtype)

    lp = lam_ref[...]
    lam = (jnp.exp(jnp.sum(lp[0:1] * lp[1:2], axis=-1, keepdims=True))
           - jnp.exp(jnp.sum(lp[2:3] * lp[3:4], axis=-1, keepdims=True)) + lam_init)
    sub = ATTN_SUB_ROWS
    for r0 in range(0, q_ref.shape[0], sub):
        q = q_ref[r0:r0 + sub, :]
        lane = lax.broadcasted_iota(jnp.int32, q.shape, 1)
        zero = jnp.zeros_like(q)
        qq = jnp.concatenate([jnp.where(lane < DIFF_QK_DIM, q, zero), jnp.where(lane >= DIFF_QK_DIM, q, zero)],
                             axis=0)
        s = _nt_dot(qq, k_ref[...])
        e = jnp.exp2(s - jnp.max(s, axis=-1, keepdims=True)).astype(BF16)
        pv = jnp.dot(e, vext_ref[...], preferred_element_type=F32)
        pv = pv[:, 0:HEAD_DIM] / pv[:, HEAD_DIM:HEAD_DIM + 1]
        out = pv[:sub] - lam * pv[sub:]
        out = out * lax.rsqrt(jnp.mean(out * out, axis=-1, keepdims=True) + LN_EPS) * g_ref[...] * (1.0 - lam_init)
        o_ref[r0:r0 + sub, :] = out.astype(o_ref.dtype)


def _diff_attention(qk, proj, lam_params, norm_g, batch, seq, lam_init, tq):
    tq = min(tq, seq)
    nq = seq // tq
    return pl.pallas_call(
        functools.partial(_diff_attn_kernel, lam_init=lam_init),
        grid=(batch, MIX_HEADS, nq),
        in_specs=[pl.BlockSpec((4, DIFF_QK_DIM), lambda b, h, i: (0, 0)),
                  pl.BlockSpec((tq, HEAD_DIM), lambda b, h, i: (b * nq + i, h)),
                  pl.BlockSpec((seq, HEAD_DIM), lambda b, h, i: (b, MIX_HEADS + h)),
                  pl.BlockSpec((seq, HEAD_DIM), lambda b, h, i: (b, 2 * MIX_HEADS + h)),
                  pl.BlockSpec((None, 1, HEAD_DIM), lambda b, h, i: (h, 0, 0))],
        out_specs=pl.BlockSpec((tq, HEAD_DIM), lambda b, h, i: (b * nq + i, h)),
        out_shape=jax.ShapeDtypeStruct((batch * seq, MIX_WIDTH), BF16),
        scratch_shapes=[pltpu.VMEM((seq, 2 * HEAD_DIM), BF16)],
        compiler_params=_params("arbitrary", "arbitrary", "arbitrary"),
        name="diff_attention",
    )(lam_params, qk, qk, proj, norm_g.reshape(MIX_HEADS, 1, HEAD_DIM))


def _mem_attn_kernel(q_ref, k_ref, v_ref, o_ref):
    s = _nt_dot(q_ref[...], k_ref[...]) * (HEAD_DIM ** -0.5)
    e = jnp.exp(s - jnp.max(s, axis=-1, keepdims=True))
    l = jnp.sum(e, axis=-1, keepdims=True)
    o_ref[...] = (jnp.dot(e.astype(BF16), v_ref[...], preferred_element_type=F32) / l).astype(o_ref.dtype)


def _mem_attention(proj, qm_block0, mem_kv, batch, seq, n_mem, tq):
    tq = min(tq, seq)
    nq = seq // tq
    return pl.pallas_call(
        _mem_attn_kernel,
        grid=(batch, MEM_HEADS, nq),
        in_specs=[pl.BlockSpec((tq, HEAD_DIM), lambda b, h, i: (b * nq + i, qm_block0 + h)),
                  pl.BlockSpec((n_mem, HEAD_DIM), lambda b, h, i: (b, h)),
                  pl.BlockSpec((n_mem, HEAD_DIM), lambda b, h, i: (b, MEM_HEADS + h))],
        out_specs=pl.BlockSpec((tq, HEAD_DIM), lambda b, h, i: (b * nq + i, h)),
        out_shape=jax.ShapeDtypeStruct((batch * seq, MEM_WIDTH), BF16),
        compiler_params=_params("parallel", "arbitrary", "arbitrary"),
        name="mem_attention",
    )(proj, mem_kv, mem_kv)


def _layer_norm_rows(z, g, b):
    mu = jnp.mean(z, axis=-1, keepdims=True)
    zc = z - mu
    var = jnp.mean(zc * zc, axis=-1, keepdims=True)
    return zc * lax.rsqrt(var + LN_EPS) * g + b


def _outproj_ln_kernel(mix_ref, mem_ref, h_ref, w_mix_ref, w_mem_ref, g_ref, b_ref, o_ref, obf_ref):
    y = (jnp.dot(mix_ref[...], w_mix_ref[...], preferred_element_type=F32)
         + jnp.dot(mem_ref[...], w_mem_ref[...], preferred_element_type=F32))
    out = _layer_norm_rows(DN_ALPHA * h_ref[...] + y, g_ref[...], b_ref[...])
    o_ref[...] = out
    obf_ref[...] = out.astype(BF16)


def _outproj_ln(mix, mem_out, h, w_out, g, b, tm):
    t, d = h.shape
    tm = min(tm, t)
    row = lambda i: (i, 0)
    fixed = lambda i: (0, 0)
    return pl.pallas_call(
        _outproj_ln_kernel,
        grid=(t // tm,),
        in_specs=[pl.BlockSpec((tm, MIX_WIDTH), row), pl.BlockSpec((tm, MEM_WIDTH), row),
                  pl.BlockSpec((tm, d), row),
                  pl.BlockSpec((MIX_WIDTH, d), fixed), pl.BlockSpec((MEM_WIDTH, d), fixed),
                  pl.BlockSpec((1, d), fixed), pl.BlockSpec((1, d), fixed)],
        out_specs=[pl.BlockSpec((tm, d), row), pl.BlockSpec((tm, d), row)],
        out_shape=[jax.ShapeDtypeStruct((t, d), F32), jax.ShapeDtypeStruct((t, d), BF16)],
        compiler_params=_params("parallel"),
        name="outproj_ln",
    )(mix, mem_out, h, w_out[:MIX_WIDTH].astype(BF16), w_out[MIX_WIDTH:].astype(BF16),
      g.reshape(1, d), b.reshape(1, d))


def _residual_ln_kernel(h_ref, f_ref, g_ref, b_ref, o_ref, obf_ref):
    out = _layer_norm_rows(DN_ALPHA * h_ref[...] + f_ref[...], g_ref[...], b_ref[...])
    o_ref[...] = out
    obf_ref[...] = out.astype(BF16)


def _residual_ln(h, f, g, b, tm):
    t, d = h.shape
    tm = min(tm, t)
    row = lambda i: (i, 0)
    fixed = lambda i: (0, 0)
    return pl.pallas_call(
        _residual_ln_kernel,
        grid=(t // tm,),
        in_specs=[pl.BlockSpec((tm, d), row), pl.BlockSpec((tm, d), row),
                  pl.BlockSpec((1, d), fixed), pl.BlockSpec((1, d), fixed)],
        out_specs=[pl.BlockSpec((tm, d), row), pl.BlockSpec((tm, d), row)],
        out_shape=[jax.ShapeDtypeStruct((t, d), F32), jax.ShapeDtypeStruct((t, d), BF16)],
        compiler_params=_params("parallel"),
        name="residual_ln",
    )(h, f, g.reshape(1, d), b.reshape(1, d))


def _lane_cumsum(x, reverse):
    lane = lax.broadcasted_iota(jnp.int32, x.shape, 1)
    sh = 1
    while sh < LANES:
        if reverse:
            x = x + jnp.where(lane < LANES - sh, pltpu.roll(x, LANES - sh, 1), 0.0)
        else:
            x = x + jnp.where(lane >= sh, pltpu.roll(x, sh, 1), 0.0)
        sh *= 2
    return x


def _log_sigmoid(x):
    return jnp.minimum(x, 0.0) - jnp.log1p(jnp.exp(-jnp.abs(x)))


def _mlstm_chunk(c, q_ref, k_ref, v_ref, ig_row, fg_row, c_ref, n_ref, m_ref, h_ref, reverse):
    L = MLSTM_CHUNK
    scale = HEAD_DIM ** -0.5
    r0 = pl.multiple_of(c * L, L)
    q = q_ref[pl.ds(r0, L), :]
    k = k_ref[pl.ds(r0, L), :]
    v = v_ref[pl.ds(r0, L), :]
    lf = _log_sigmoid(fg_row)
    b_row8 = _lane_cumsum(lf, reverse)
    b_last = jnp.sum(lf[0:1], axis=-1, keepdims=True)
    b_row = jnp.broadcast_to(b_row8[0:1], (L, L))
    ig_rowb = jnp.broadcast_to(ig_row[0:1], (L, L))
    b_col = b_row.T
    ig_col = ig_rowb.T
    jj = lax.broadcasted_iota(jnp.int32, (L, L), 0)
    ss = lax.broadcasted_iota(jnp.int32, (L, L), 1)
    mask = (ss >= jj) if reverse else (ss <= jj)
    dm = jnp.where(mask, b_col - b_row + ig_rowb, NEG_INF)
    m_old = m_ref[...]
    inter = b_col[:, 0:1] + m_old
    m_j = jnp.maximum(inter, jnp.max(dm, axis=-1, keepdims=True))
    wd = jnp.exp(dm - m_j)
    wi = jnp.exp(inter - m_j)
    qk = _nt_dot(q, k) * scale * wd
    c_old = c_ref[...]
    n_old = n_ref[...]
    num = (wi * jnp.dot(q, c_old.astype(BF16), preferred_element_type=F32)
           + jnp.dot(qk.astype(BF16), v, preferred_element_type=F32))
    den = wi * jnp.sum(q.astype(F32) * n_old, axis=-1, keepdims=True) + jnp.sum(qk, axis=-1, keepdims=True)
    h_ref[pl.ds(r0, L), :] = num / jnp.maximum(jnp.abs(den), jnp.exp(-m_j))
    a_col = b_last - b_col[:, 0:1] + ig_col[:, 0:1]
    a_max = jnp.max(a_col, axis=0, keepdims=True)
    wa = jnp.exp(a_col - a_max)
    kf = k.astype(F32)
    vw = (v.astype(F32) * wa).astype(BF16)
    kv = jnp.dot(kf.T.astype(BF16), vw, preferred_element_type=F32) * scale
    kc = jnp.sum(kf * wa, axis=0, keepdims=True) * scale
    m_new = jnp.maximum(b_last + m_old, a_max)
    decay = jnp.exp(b_last + m_old - m_new)
    w = jnp.exp(a_max - m_new)
    c_ref[...] = decay * c_old + w * kv
    n_ref[...] = decay * n_old + w * kc
    m_ref[...] = m_new


def _mlstm_kernel(bias_ref, q_ref, k_ref, v_ref, o_ref, g_ref, ng_ref, out_ref,
                  hf_ref, hb_ref, cf_ref, cb_ref, nf_ref, nb_ref, mf_ref, mb_ref):
    hd = pl.program_id(1)
    nc = q_ref.shape[0] // MLSTM_CHUNK
    for r in (cf_ref, cb_ref, nf_ref, nb_ref, mf_ref, mb_ref):
        r[...] = jnp.zeros_like(r)

    def gate_row(kind, c):
        row = g_ref[kind, pl.ds(c, 1), :] + bias_ref[kind, hd]
        return jnp.broadcast_to(row, (SUBLANES, MLSTM_CHUNK))

    def body(c2, carry):
        for u in range(MLSTM_UNROLL):
            c = c2 * MLSTM_UNROLL + u
            _mlstm_chunk(c, q_ref, k_ref, v_ref, gate_row(0, c), gate_row(1, c),
                         cf_ref, nf_ref, mf_ref, hf_ref, False)
            cr = nc - 1 - c
            _mlstm_chunk(cr, q_ref, k_ref, v_ref, gate_row(2, cr), gate_row(3, cr),
                         cb_ref, nb_ref, mb_ref, hb_ref, True)
        return carry

    lax.fori_loop(0, nc // MLSTM_UNROLL, body, 0)
    hs = hf_ref[...] + hb_ref[...]
    hs = hs * lax.rsqrt(jnp.mean(hs * hs, axis=-1, keepdims=True) + LN_EPS) * ng_ref[...]
    out_ref[...] = (jax.nn.sigmoid(o_ref[...].astype(F32)) * hs).astype(out_ref.dtype)


def _mlstm(proj, gates_t, gate_bias, norm_g, batch, seq):
    nc = seq // MLSTM_CHUNK
    col = lambda off: pl.BlockSpec((seq, HEAD_DIM), lambda b, h: (b, off + h))
    st = lambda shape: pltpu.VMEM(shape, F32)
    return pl.pallas_call(
        _mlstm_kernel,
        grid=(batch, MIX_HEADS),
        in_specs=[pl.BlockSpec(memory_space=pltpu.SMEM),
                  col(0), col(MIX_HEADS), col(2 * MIX_HEADS), col(3 * MIX_HEADS),
                  pl.BlockSpec((None, 4, None, nc, MLSTM_CHUNK), lambda b, h: (b, 0, h, 0, 0)),
                  pl.BlockSpec((None, 1, HEAD_DIM), lambda b, h: (h, 0, 0))],
        out_specs=pl.BlockSpec((seq, HEAD_DIM), lambda b, h: (b, h)),
        out_shape=jax.ShapeDtypeStruct((batch * seq, MIX_WIDTH), BF16),
        scratch_shapes=[st((seq, HEAD_DIM)), st((seq, HEAD_DIM)),
                        st((HEAD_DIM, HEAD_DIM)), st((HEAD_DIM, HEAD_DIM)),
                        st((1, HEAD_DIM)), st((1, HEAD_DIM)), st((1, 1)), st((1, 1))],
        compiler_params=_params("parallel", "arbitrary"),
        name="mlstm",
    )(gate_bias, proj, proj, proj, proj, gates_t, norm_g.reshape(MIX_HEADS, 1, HEAD_DIM))


def _topk_rows(vals, n_rows, payload=None):
    row = lax.broadcasted_iota(jnp.int32, vals.shape, 0)
    top_v, top_i = [], []
    for _ in range(PEER_TOPK):
        mx = jnp.max(vals, axis=0, keepdims=True)
        pos = jnp.min(jnp.where(vals == mx, row, n_rows), axis=0, keepdims=True)
        hit = row == pos
        top_v.append(mx)
        top_i.append(pos if payload is None else jnp.max(jnp.where(hit, payload, -1), axis=0, keepdims=True))
        vals = jnp.where(hit, NEG_INF, vals)
    return jnp.concatenate(top_v, axis=0), jnp.concatenate(top_i, axis=0)


def _peer_subkey_topk_kernel(q_ref, sk_ref, sv_ref, si_ref):
    s_t = _nt_dot(sk_ref[...], q_ref[...])
    sv, si = _topk_rows(s_t, PEER_NKEYS)
    sv_ref[...] = sv
    si_ref[...] = si


def _peer_expert_topk_kernel(sv0_ref, si0_ref, sv1_ref, si1_ref, idx_ref, gate_ref):
    sv0, si0, sv1, si1 = sv0_ref[...], si0_ref[...], sv1_ref[...], si1_ref[...]
    k = PEER_TOPK
    row8 = lax.broadcasted_iota(jnp.int32, (SUBLANES, sv0.shape[1]), 0)
    cand = [sv0[0:1] + sv1]
    cidx = [si0[0:1] * PEER_NKEYS + si1]
    for i in range(1, SUBLANES):
        cand.append(jnp.where(row8 < k // (i + 1), sv0[i:i + 1] + sv1[0:SUBLANES], NEG_INF))
        cidx.append(si0[i:i + 1] * PEER_NKEYS + si1[0:SUBLANES])
    cand.append(sv0[SUBLANES:k] + sv1[0:1])
    cidx.append(si0[SUBLANES:k] * PEER_NKEYS + si1[0:1])
    cand = jnp.concatenate(cand, axis=0)
    cidx = jnp.concatenate(cidx, axis=0)
    top_s, eidx = _topk_rows(cand, cand.shape[0], payload=cidx)
    e = jnp.exp(top_s - jnp.max(top_s, axis=0, keepdims=True))
    idx_ref[...] = eidx
    gate_ref[...] = e / jnp.sum(e, axis=0, keepdims=True)


def _peer_route(qry, sub_keys, tm):
    t = qry.shape[0]
    tm = min(tm, t)
    nhp = 2 * PEER_HEADS
    half = pl.BlockSpec((None, PEER_TOPK, tm), lambda i, j: (j, 0, i))
    sv, si = pl.pallas_call(
        _peer_subkey_topk_kernel,
        grid=(t // tm, nhp),
        in_specs=[pl.BlockSpec((tm, PEER_SUBK), lambda i, j: (i, j)),
                  pl.BlockSpec((None, PEER_NKEYS, PEER_SUBK), lambda i, j: (j, 0, 0))],
        out_specs=[half, half],
        out_shape=[jax.ShapeDtypeStruct((nhp, PEER_TOPK, t), F32),
                   jax.ShapeDtypeStruct((nhp, PEER_TOPK, t), jnp.int32)],
        compiler_params=_params("parallel", "arbitrary"),
        name="peer_subkey_topk",
    )(qry, sub_keys.reshape(nhp, PEER_NKEYS, PEER_SUBK).astype(BF16))
    first = pl.BlockSpec((None, PEER_TOPK, tm), lambda i, j: (2 * j, 0, i))
    second = pl.BlockSpec((None, PEER_TOPK, tm), lambda i, j: (2 * j + 1, 0, i))
    head = pl.BlockSpec((None, PEER_TOPK, tm), lambda i, j: (j, 0, i))
    eidx, gate = pl.pallas_call(
        _peer_expert_topk_kernel,
        grid=(t // tm, PEER_HEADS),
        in_specs=[first, first, second, second],
        out_specs=[head, head],
        out_shape=[jax.ShapeDtypeStruct((PEER_HEADS, PEER_TOPK, t), jnp.int32),
                   jax.ShapeDtypeStruct((PEER_HEADS, PEER_TOPK, t), F32)],
        compiler_params=_params("parallel", "arbitrary"),
        name="peer_expert_topk",
    )(sv, si, sv, si)
    n_sel = PEER_HEADS * PEER_TOPK
    return eidx.reshape(n_sel, t).T, gate.reshape(n_sel, t).T


N_SEL = PEER_HEADS * PEER_TOPK
GATHER_TOKENS = 8
GATHER_AHEAD = 2
GATHER_SLOTS = GATHER_AHEAD + 1
GATHER_PIPE = 2


def _gelu(x):
    return 0.5 * x * (1.0 + lax.erf(x * (2.0 ** -0.5)))


def _peer_gather_kernel(idx_ref, idx_next_ref, idx_ahead_ref, x_ref, gate_ref, uv_hbm, f_ref,
                        buf_ring, sem, p_ref, w_ref):
    tb = x_ref.shape[0]
    rows = x_ref.shape[1]
    i = pl.program_id(0)
    slot = i % GATHER_SLOTS

    def start_row(ids_ref, dst_slot, t, k):
        pltpu.make_async_copy(uv_hbm.at[ids_ref[t, k]], buf_ring.at[dst_slot, k, t],
                              sem.at[dst_slot]).start(priority=k % 2)

    def wait_slot(s):
        pltpu.make_async_copy(buf_ring.at[s], buf_ring.at[s], sem.at[s]).wait()

    @pl.when(i == 0)
    def _():
        for b, ids_ref in enumerate((idx_ref, idx_next_ref)):
            def per_token(t, carry, b=b, ids_ref=ids_ref):
                for k in range(N_SEL):
                    start_row(ids_ref, b, t, k)
                return carry
            lax.fori_loop(0, tb, per_token, 0)

    u_fetches = [k for k in range(N_SEL) if k % 2 == 0]
    v_fetches = [k for k in range(N_SEL) if k % 2 == 0]
    act_fetches = N_SEL - len(u_fetches) - len(v_fetches)

    def evaluate(buf, ahead_slot):
        def per_token(t, carry):
            x = x_ref[t].astype(F32)
            n = 0
            for k in range(N_SEL):
                if k in u_fetches:
                    start_row(idx_ahead_ref, ahead_slot, t, n)
                    n += 1
                p = buf[k, t, 0:rows, :].astype(F32) * x
                part = p[0:SUBLANES]
                for r in range(1, rows // SUBLANES):
                    part = part + p[r * SUBLANES:(r + 1) * SUBLANES]
                p_ref[pl.ds(k * SUBLANES, SUBLANES), :] = part
            for _ in range(act_fetches):
                start_row(idx_ahead_ref, ahead_slot, t, n)
                n += 1
            s = p_ref[pl.ds(0, N_SEL, stride=SUBLANES), :]
            for r in range(1, SUBLANES):
                s = s + p_ref[pl.ds(r, N_SEL, stride=SUBLANES), :]
            act = jnp.sum(s, axis=-1, keepdims=True)
            gate_col = jnp.broadcast_to(gate_ref[pl.ds(t, 1), :], (N_SEL, N_SEL)).T
            w_ref[...] = gate_col * _gelu(act)
            acc = jnp.zeros((rows, LANES), F32)
            for k in range(N_SEL):
                if k in v_fetches:
                    start_row(idx_ahead_ref, ahead_slot, t, n)
                    n += 1
                acc = acc + w_ref[pl.ds(k, 1), :] * buf[k, t, rows:2 * rows, :].astype(F32)
            f_ref[t] = acc
            return carry

        lax.fori_loop(0, tb, per_token, 0)

    wait_slot(slot)
    evaluate(buf_ring.at[slot], (i + GATHER_AHEAD) % GATHER_SLOTS)

    @pl.when(i == pl.num_programs(0) - 1)
    def _():
        for a in range(1, GATHER_AHEAD + 1):
            wait_slot((i + a) % GATHER_SLOTS)


def _peer_experts(x_bf, eidx, gate, uv):
    t, d = x_bf.shape
    rows = d // LANES
    tb = GATHER_TOKENS
    assert t % tb == 0
    xs = x_bf.reshape(t, rows, LANES)
    nblk = t // tb
    idx_spec = lambda ahead: pl.BlockSpec((tb, N_SEL), lambda i: (jnp.minimum(i + ahead, nblk - 1), 0),
                                          memory_space=pltpu.SMEM)
    f = pl.pallas_call(
        _peer_gather_kernel,
        grid=(nblk,),
        in_specs=[idx_spec(0), idx_spec(1), idx_spec(GATHER_AHEAD),
                  pl.BlockSpec((tb, rows, LANES), lambda i: (i, 0, 0)),
                  pl.BlockSpec((tb, N_SEL), lambda i: (i, 0)),
                  pl.BlockSpec(memory_space=pl.ANY)],
        out_specs=pl.BlockSpec((tb, rows, LANES), lambda i: (i, 0, 0)),
        out_shape=jax.ShapeDtypeStruct((t, rows, LANES), F32),
        scratch_shapes=[pltpu.VMEM((GATHER_SLOTS, N_SEL, tb, 2 * rows, LANES), BF16),
                        pltpu.SemaphoreType.DMA((GATHER_SLOTS,)),
                        pltpu.VMEM((N_SEL * SUBLANES, LANES), F32),
                        pltpu.VMEM((N_SEL, LANES), F32)],
        compiler_params=_params("arbitrary"),
        name="peer_experts",
    )(eidx, eidx, eidx, xs, gate, uv)
    return f.reshape(t, d)


def _peer_ffn(h, h_bf, w_query, sub_keys, u_tab, v_tab, ln_g, ln_b):
    t, d = h.shape
    rows = d // LANES
    n_exp = u_tab.shape[0]
    qry = _matmul(h_bf, w_query.astype(BF16), BF16, 1024, 512)
    eidx, gate = _peer_route(qry, sub_keys, 256)
    uv = jnp.concatenate([u_tab.reshape(n_exp, rows, LANES), v_tab.reshape(n_exp, rows, LANES)],
                         axis=1).astype(BF16)
    f = _peer_experts(h_bf, eidx, gate, uv)
    return _residual_ln(h, f, ln_g, ln_b, 256)


def _rope_tables(positions):
    half = DIFF_QK_DIM // 2
    inv = jnp.power(ROPE_THETA, -jnp.arange(half, dtype=F32) * (2.0 / DIFF_QK_DIM))
    ang = positions.astype(F32).reshape(-1, 1) * inv
    cos, sin = jnp.cos(ang), jnp.sin(ang)
    return (jnp.concatenate([cos, cos, cos, cos], axis=-1),
            jnp.concatenate([-sin, sin, -sin, sin], axis=-1))


def kernel(x, mem, positions, attn_w_in, attn_lambda, attn_head_norm, mlstm_w_in, mlstm_gate_bias, mlstm_head_norm, mem_w_kv, w_out, ln_mix_g, ln_mix_b, peer_w_query, peer_sub_keys, peer_u, peer_v, ln_ffn_g, ln_ffn_b):
    batch, seq, d = x.shape
    n_mem = mem.shape[1]
    t = batch * seq
    mem_kv = _matmul(mem.reshape(batch * n_mem, d).astype(BF16), mem_w_kv.astype(BF16), BF16, 1024, 512)
    cos_t, sin_t = _rope_tables(positions)
    h = x.reshape(t, d)
    h_bf = h.astype(BF16)
    for i in range(DEPTH):
        j = i // 2
        if i % 2 == 0:
            proj = _matmul(h_bf, attn_w_in[j].astype(BF16), BF16, 1024, 512)
            qk = _rope(proj, cos_t, sin_t, 1024)
            lam_init = 0.8 - 0.6 * math.exp(-0.3 * i)
            mix = _diff_attention(qk, proj, attn_lambda[j], attn_head_norm[j], batch, seq, lam_init, 512)
            qm_block0 = 3 * MIX_HEADS
        else:
            n_main = 4 * MIX_WIDTH + MEM_WIDTH
            w_in = mlstm_w_in[j]
            proj = _matmul(h_bf, w_in[:, :n_main].astype(BF16), BF16, 1024, 512)
            n_gates = 4 * MIX_HEADS
            w_gates = jnp.pad(w_in[:, n_main:], ((0, 0), (0, LANES - n_gates))).astype(BF16)
            gates = _matmul(h_bf, w_gates, F32, 1024, LANES)[:, :n_gates]
            gates_t = gates.reshape(batch, seq // MLSTM_CHUNK, MLSTM_CHUNK, 4, MIX_HEADS).transpose(0, 3, 4, 1, 2)
            mix = _mlstm(proj, gates_t, mlstm_gate_bias[j], mlstm_head_norm[j], batch, seq)
            qm_block0 = 4 * MIX_HEADS
        mem_out = _mem_attention(proj, qm_block0, mem_kv, batch, seq, n_mem, 512)
        h, h_bf = _outproj_ln(mix, mem_out, h, w_out[i], ln_mix_g[i], ln_mix_b[i], 256)
        h, h_bf = _peer_ffn(h, h_bf, peer_w_query[i], peer_sub_keys[i], peer_u[i], peer_v[i],
                            ln_ffn_g[i], ln_ffn_b[i])
    return h.reshape(batch, seq, d)
```

```python
import functools
import math

import jax
import jax.numpy as jnp
from jax import lax
from jax.experimental import pallas as pl
from jax.experimental.pallas import tpu as pltpu

F32 = jnp.float32
BF16 = jnp.bfloat16

HEAD_DIM = 128
MIX_HEADS = 12
MIX_WIDTH = MIX_HEADS * HEAD_DIM
MEM_HEADS = 4
MEM_WIDTH = MEM_HEADS * HEAD_DIM
DIFF_QK_DIM = HEAD_DIM // 2
ROPE_THETA = 10000.0
MLSTM_CHUNK = 128
PEER_HEADS = 8
PEER_NKEYS = 128
PEER_TOPK = 16
PEER_SUBK = 128
DEPTH = 2
DN_ALPHA = (2.0 * DEPTH) ** 0.25
LN_EPS = 1e-5

LANES = 128
SUBLANES = 8
VMEM_LIMIT_BYTES = 48 * 1024 * 1024
NEG_INF = float("-inf")
LOG2_E = math.log2(math.e)
ATTN_SUB_ROWS = 128
MLSTM_UNROLL = 4


def _params(*sem):
    return pltpu.CompilerParams(dimension_semantics=sem, vmem_limit_bytes=VMEM_LIMIT_BYTES)


def _nt_dot(a, b):
    return lax.dot_general(a, b, (((1,), (1,)), ((), ())), preferred_element_type=F32)


def _matmul_kernel(a_ref, b_ref, o_ref):
    o_ref[...] = jnp.dot(a_ref[...], b_ref[...], preferred_element_type=F32).astype(o_ref.dtype)


def _matmul(a, b, out_dtype, tm, tn):
    m, k = a.shape
    n = b.shape[1]
    tm, tn = min(tm, m), min(tn, n)
    return pl.pallas_call(
        _matmul_kernel,
        grid=(m // tm, n // tn),
        in_specs=[pl.BlockSpec((tm, k), lambda i, j: (i, 0)),
                  pl.BlockSpec((k, tn), lambda i, j: (0, j))],
        out_specs=pl.BlockSpec((tm, tn), lambda i, j: (i, j)),
        out_shape=jax.ShapeDtypeStruct((m, n), out_dtype),
        compiler_params=_params("parallel", "arbitrary"),
        name="matmul",
    )(a, b)


def _rope_kernel(x_ref, cos_ref, sin_ref, o_ref):
    cos, sin = cos_ref[...], sin_ref[...]
    lane = lax.broadcasted_iota(jnp.int32, cos.shape, 1)
    first = (lane % DIFF_QK_DIM) < (DIFF_QK_DIM // 2)
    scale = jnp.where(pl.program_id(1) == 0, DIFF_QK_DIM ** -0.5 * LOG2_E, 1.0).astype(F32)
    for h in range(MIX_HEADS):
        cols = slice(h * HEAD_DIM, (h + 1) * HEAD_DIM)
        x = x_ref[:, cols].astype(F32)
        rot = jnp.where(first, pltpu.roll(x, LANES - DIFF_QK_DIM // 2, 1), pltpu.roll(x, DIFF_QK_DIM // 2, 1))
        o_ref[:, cols] = ((x * cos + rot * sin) * scale).astype(o_ref.dtype)


def _rope(proj, cos_t, sin_t, tm):
    t = proj.shape[0]
    tm = min(tm, t)
    blk = pl.BlockSpec((tm, MIX_WIDTH), lambda i, j: (i, j))
    tab = pl.BlockSpec((tm, LANES), lambda i, j: (i, 0))
    return pl.pallas_call(
        _rope_kernel,
        grid=(t // tm, 2),
        in_specs=[blk, tab, tab],
        out_specs=blk,
        out_shape=jax.ShapeDtypeStruct((t, 2 * MIX_WIDTH), BF16),
        compiler_params=_params("parallel", "arbitrary"),
        name="rope",
    )(proj, cos_t, sin_t)


def _diff_attn_kernel(lam_ref, q_ref, k_ref, v_ref, g_ref, o_ref, vext_ref, *, lam_init):
    @pl.when(pl.program_id(2) == 0)
    def _():
        vext_ref[:, 0:HEAD_DIM] = v_ref[...]
        vext_ref[:, HEAD_DIM:2 * HEAD_DIM] = jnp.ones(v_ref.shape, v_ref.dtype)

    lp = lam_ref[...]
    lam = (jnp.exp(jnp.sum(lp[0:1] * lp[1:2], axis=-1, keepdims=True))
           - jnp.exp(jnp.sum(lp[2:3] * lp[3:4], axis=-1, keepdims=True)) + lam_init)
    sub = ATTN_SUB_ROWS
    for r0 in range(0, q_ref.shape[0], sub):
        q = q_ref[r0:r0 + sub, :]
        lane = lax.broadcasted_iota(jnp.int32, q.shape, 1)
        zero = jnp.zeros_like(q)
        qq = jnp.concatenate([jnp.where(lane < DIFF_QK_DIM, q, zero), jnp.where(lane >= DIFF_QK_DIM, q, zero)],
                             axis=0)
        s = _nt_dot(qq, k_ref[...])
        e = jnp.exp2(s - jnp.max(s, axis=-1, keepdims=True)).astype(BF16)
        pv = jnp.dot(e, vext_ref[...], preferred_element_type=F32)
        pv = pv[:, 0:HEAD_DIM] / pv[:, HEAD_DIM:HEAD_DIM + 1]
        out = pv[:sub] - lam * pv[sub:]
        out = out * lax.rsqrt(jnp.mean(out * out, axis=-1, keepdims=True) + LN_EPS) * g_ref[...] * (1.0 - lam_init)
        o_ref[r0:r0 + sub, :] = out.astype(o_ref.dtype)


def _diff_attention(qk, proj, lam_params, norm_g, batch, seq, lam_init, tq):
    tq = min(tq, seq)
    nq = seq // tq
    return pl.pallas_call(
        functools.partial(_diff_attn_kernel, lam_init=lam_init),
        grid=(batch, MIX_HEADS, nq),
        in_specs=[pl.BlockSpec((4, DIFF_QK_DIM), lambda b, h, i: (0, 0)),
                  pl.BlockSpec((tq, HEAD_DIM), lambda b, h, i: (b * nq + i, h)),
                  pl.BlockSpec((seq, HEAD_DIM), lambda b, h, i: (b, MIX_HEADS + h)),
                  pl.BlockSpec((seq, HEAD_DIM), lambda b, h, i: (b, 2 * MIX_HEADS + h)),
                  pl.BlockSpec((None, 1, HEAD_DIM), lambda b, h, i: (h, 0, 0))],
        out_specs=pl.BlockSpec((tq, HEAD_DIM), lambda b, h, i: (b * nq + i, h)),
        out_shape=jax.ShapeDtypeStruct((batch * seq, MIX_WIDTH), BF16),
        scratch_shapes=[pltpu.VMEM((seq, 2 * HEAD_DIM), BF16)],
        compiler_params=_params("arbitrary", "arbitrary", "arbitrary"),
        name="diff_attention",
    )(lam_params, qk, qk, proj, norm_g.reshape(MIX_HEADS, 1, HEAD_DIM))


def _mem_attn_kernel(q_ref, k_ref, v_ref, o_ref):
    s = _nt_dot(q_ref[...], k_ref[...]) * (HEAD_DIM ** -0.5)
    e = jnp.exp(s - jnp.max(s, axis=-1, keepdims=True))
    l = jnp.sum(e, axis=-1, keepdims=True)
    o_ref[...] = (jnp.dot(e.astype(BF16), v_ref[...], preferred_element_type=F32) / l).astype(o_ref.dtype)


def _mem_attention(proj, qm_block0, mem_kv, batch, seq, n_mem, tq):
    tq = min(tq, seq)
    nq = seq // tq
    return pl.pallas_call(
        _mem_attn_kernel,
        grid=(batch, MEM_HEADS, nq),
        in_specs=[pl.BlockSpec((tq, HEAD_DIM), lambda b, h, i: (b * nq + i, qm_block0 + h)),
                  pl.BlockSpec((n_mem, HEAD_DIM), lambda b, h, i: (b, h)),
                  pl.BlockSpec((n_mem, HEAD_DIM), lambda b, h, i: (b, MEM_HEADS + h))],
        out_specs=pl.BlockSpec((tq, HEAD_DIM), lambda b, h, i: (b * nq + i, h)),
        out_shape=jax.ShapeDtypeStruct((batch * seq, MEM_WIDTH), BF16),
        compiler_params=_params("parallel", "arbitrary", "arbitrary"),
        name="mem_attention",
    )(proj, mem_kv, mem_kv)


def _layer_norm_rows(z, g, b):
    mu = jnp.mean(z, axis=-1, keepdims=True)
    zc = z - mu
    var = jnp.mean(zc * zc, axis=-1, keepdims=True)
    return zc * lax.rsqrt(var + LN_EPS) * g + b


def _outproj_ln_kernel(mix_ref, mem_ref, h_ref, w_mix_ref, w_mem_ref, g_ref, b_ref, o_ref, obf_ref):
    y = (jnp.dot(mix_ref[...], w_mix_ref[...], preferred_element_type=F32)
         + jnp.dot(mem_ref[...], w_mem_ref[...], preferred_element_type=F32))
    out = _layer_norm_rows(DN_ALPHA * h_ref[...] + y, g_ref[...], b_ref[...])
    o_ref[...] = out
    obf_ref[...] = out.astype(BF16)


def _outproj_ln(mix, mem_out, h, w_out, g, b, tm):
    t, d = h.shape
    tm = min(tm, t)
    row = lambda i: (i, 0)
    fixed = lambda i: (0, 0)
    return pl.pallas_call(
        _outproj_ln_kernel,
        grid=(t // tm,),
        in_specs=[pl.BlockSpec((tm, MIX_WIDTH), row), pl.BlockSpec((tm, MEM_WIDTH), row),
                  pl.BlockSpec((tm, d), row),
                  pl.BlockSpec((MIX_WIDTH, d), fixed), pl.BlockSpec((MEM_WIDTH, d), fixed),
                  pl.BlockSpec((1, d), fixed), pl.BlockSpec((1, d), fixed)],
        out_specs=[pl.BlockSpec((tm, d), row), pl.BlockSpec((tm, d), row)],
        out_shape=[jax.ShapeDtypeStruct((t, d), F32), jax.ShapeDtypeStruct((t, d), BF16)],
        compiler_params=_params("parallel"),
        name="outproj_ln",
    )(mix, mem_out, h, w_out[:MIX_WIDTH].astype(BF16), w_out[MIX_WIDTH:].astype(BF16),
      g.reshape(1, d), b.reshape(1, d))


def _residual_ln_kernel(h_ref, f_ref, g_ref, b_ref, o_ref, obf_ref):
    out = _layer_norm_rows(DN_ALPHA * h_ref[...] + f_ref[...], g_ref[...], b_ref[...])
    o_ref[...] = out
    obf_ref[...] = out.astype(BF16)


def _residual_ln(h, f, g, b, tm):
    t, d = h.shape
    tm = min(tm, t)
    row = lambda i: (i, 0)
    fixed = lambda i: (0, 0)
    return pl.pallas_call(
        _residual_ln_kernel,
        grid=(t // tm,),
        in_specs=[pl.BlockSpec((tm, d), row), pl.BlockSpec((tm, d), row),
                  pl.BlockSpec((1, d), fixed), pl.BlockSpec((1, d), fixed)],
        out_specs=[pl.BlockSpec((tm, d), row), pl.BlockSpec((tm, d), row)],
        out_shape=[jax.ShapeDtypeStruct((t, d), F32), jax.ShapeDtypeStruct((t, d), BF16)],
        compiler_params=_params("parallel"),
        name="residual_ln",
    )(h, f, g.reshape(1, d), b.reshape(1, d))


def _lane_cumsum(x, reverse):
    lane = lax.broadcasted_iota(jnp.int32, x.shape, 1)
    sh = 1
    while sh < LANES:
        if reverse:
            x = x + jnp.where(lane < LANES - sh, pltpu.roll(x, LANES - sh, 1), 0.0)
        else:
            x = x + jnp.where(lane >= sh, pltpu.roll(x, sh, 1), 0.0)
        sh *= 2
    return x


def _log_sigmoid(x):
    return jnp.minimum(x, 0.0) - jnp.log1p(jnp.exp(-jnp.abs(x)))


def _mlstm_kernel(bias_ref, q_ref, k_ref, v_ref, o_ref, g_ref, ng_ref, out_ref,
                  brow_ref, ig_ref, blast_ref, amax_ref, ms_ref, inner_ref, mi_ref, bcol_ref, kv_ref, cs_ref,
                  state_ref, m_ref):
    L = MLSTM_CHUNK
    D = HEAD_DIM
    scale = HEAD_DIM ** -0.5
    hd = pl.program_id(1)
    nc = q_ref.shape[0] // L
    jj = lax.broadcasted_iota(jnp.int32, (L, L), 0)
    ss = lax.broadcasted_iota(jnp.int32, (L, L), 1)
    masks = (ss <= jj, ss >= jj)
    ones_blk = jnp.ones((L, D), BF16)

    for d in range(2):
        lf = _log_sigmoid(g_ref[2 * d + 1] + bias_ref[2 * d + 1, hd])
        brow_ref[d] = _lane_cumsum(lf, reverse=(d == 1))
        blast_ref[d] = jnp.broadcast_to(jnp.sum(lf, axis=-1, keepdims=True), (nc, L))
        ig_ref[d] = g_ref[2 * d] + bias_ref[2 * d, hd]

    def pass1(c, carry):
        r0 = pl.multiple_of(c * L, L)
        rows = pl.ds(r0, L)
        q = q_ref[rows, :]
        k = k_ref[rows, :]
        v = v_ref[rows, :]
        kt = k.astype(F32).T.astype(BF16)
        qk = _nt_dot(q, k) * scale
        for d in range(2):
            b_row = jnp.broadcast_to(brow_ref[d, pl.ds(c, 1), :], (L, L))
            ig_row = jnp.broadcast_to(ig_ref[d, pl.ds(c, 1), :], (L, L))
            b_col = b_row.T
            ig_col = ig_row.T
            dm = jnp.where(masks[d], b_col - b_row + ig_row, NEG_INF)
            m_intra = jnp.max(dm, axis=-1, keepdims=True)
            s = (qk * jnp.exp(dm - m_intra)).astype(BF16)
            inner_ref[d, rows, :] = jnp.dot(s, jnp.concatenate([v, ones_blk], axis=1), preferred_element_type=F32)
            mi_ref[d, rows, :] = jnp.broadcast_to(m_intra, (L, D))
            bcol_ref[d, rows, :] = b_col
            a = blast_ref[d, pl.ds(c, 1), :] - b_col + ig_col
            a_max = jnp.max(a, axis=0, keepdims=True)
            wa = jnp.exp(a - a_max)
            vw = jnp.concatenate([v.astype(F32) * wa, wa], axis=1).astype(BF16)
            kv_ref[d, c] = jnp.dot(kt, vw, preferred_element_type=F32) * scale
            amax_ref[d, pl.ds(c, 1), :] = a_max
        return carry

    unroll = math.gcd(nc, MLSTM_UNROLL)

    def unrolled(body):
        def group(g, carry):
            for u in range(unroll):
                carry = body(g * unroll + u, carry)
            return carry
        return group

    lax.fori_loop(0, nc // unroll, unrolled(pass1), 0)

    state_ref[...] = jnp.zeros_like(state_ref)
    m_ref[...] = jnp.zeros_like(m_ref)

    def pass2(step, carry):
        for d in range(2):
            c = step if d == 0 else nc - 1 - step
            st = state_ref[d]
            m_old = m_ref[d]
            cs_ref[d, c] = st.astype(BF16)
            ms_ref[d, pl.ds(c, 1), :] = m_old
            b_last = blast_ref[d, pl.ds(c, 1), :]
            a_max = amax_ref[d, pl.ds(c, 1), :]
            m_new = jnp.maximum(b_last + m_old, a_max)
            decay = jnp.exp(b_last + m_old - m_new)
            w = jnp.exp(a_max - m_new)
            wide = lambda x: jnp.concatenate([x, x], axis=1)
            state_ref[d] = wide(decay) * st + wide(w) * kv_ref[d, c]
            m_ref[d] = m_new
        return carry

    lax.fori_loop(0, nc, pass2, 0)

    def pass3(c, carry):
        r0 = pl.multiple_of(c * L, L)
        rows = pl.ds(r0, L)
        q = q_ref[rows, :]
        hs = jnp.zeros((L, D), F32)
        for d in range(2):
            qc = jnp.dot(q, cs_ref[d, c], preferred_element_type=F32)
            inter = bcol_ref[d, rows, :] + ms_ref[d, pl.ds(c, 1), :]
            mi = mi_ref[d, rows, :]
            m_j = jnp.maximum(inter, mi)
            fi = jnp.exp(mi - m_j)
            wi = jnp.exp(inter - m_j)
            inn = inner_ref[d, rows, :]
            num = wi * qc[:, 0:D] + fi * inn[:, 0:D]
            den = wi * qc[:, D:2 * D] + fi * inn[:, D:2 * D]
            hs = hs + num / jnp.maximum(jnp.abs(den), jnp.exp(-m_j))
        hs = hs * lax.rsqrt(jnp.mean(hs * hs, axis=-1, keepdims=True) + LN_EPS) * ng_ref[...]
        out_ref[rows, :] = (jax.nn.sigmoid(o_ref[rows, :].astype(F32)) * hs).astype(out_ref.dtype)
        return carry

    lax.fori_loop(0, nc // unroll, unrolled(pass3), 0)


def _mlstm(proj, gates_t, gate_bias, norm_g, batch, seq):
    L, D = MLSTM_CHUNK, HEAD_DIM
    nc = seq // L
    col = lambda off: pl.BlockSpec((seq, D), lambda b, h: (b, off + h))
    st = lambda shape, dtype=F32: pltpu.VMEM(shape, dtype)
    return pl.pallas_call(
        _mlstm_kernel,
        grid=(batch, MIX_HEADS),
        in_specs=[pl.BlockSpec(memory_space=pltpu.SMEM),
                  col(0), col(MIX_HEADS), col(2 * MIX_HEADS), col(3 * MIX_HEADS),
                  pl.BlockSpec((None, 4, None, nc, L), lambda b, h: (b, 0, h, 0, 0)),
                  pl.BlockSpec((None, 1, D), lambda b, h: (h, 0, 0))],
        out_specs=pl.BlockSpec((seq, D), lambda b, h: (b, h)),
        out_shape=jax.ShapeDtypeStruct((batch * seq, MIX_WIDTH), BF16),
        scratch_shapes=[st((2, nc, L)), st((2, nc, L)), st((2, nc, L)), st((2, nc, L)), st((2, nc, L)),
                        st((2, seq, 2 * D)), st((2, seq, D)), st((2, seq, D)),
                        st((2, nc, D, 2 * D)), st((2, nc, D, 2 * D), BF16),
                        st((2, D, 2 * D)), st((2, 1, L))],
        compiler_params=_params("parallel", "arbitrary"),
        name="mlstm",
    )(gate_bias, proj, proj, proj, proj, gates_t, norm_g.reshape(MIX_HEADS, 1, HEAD_DIM))


def _topk_rows(vals, n_rows, payload=None):
    tm = vals.shape[1]
    row = lax.broadcasted_iota(jnp.int32, vals.shape, 0)
    row8 = lax.broadcasted_iota(jnp.int32, (SUBLANES, tm), 0)

    def better(a, b):
        keep = a[0] >= b[0]
        return tuple(jnp.where(keep, x, y) for x, y in zip(a, b))

    top_v, top_i = [], []
    for _ in range(PEER_TOPK):
        items = []
        for g in range(n_rows // SUBLANES):
            sl = slice(g * SUBLANES, (g + 1) * SUBLANES)
            items.append((vals[sl], row8 + g * SUBLANES) + (() if payload is None else (payload[sl],)))
        while len(items) > 1:
            nxt = [better(items[j], items[j + 1]) for j in range(0, len(items) - 1, 2)]
            items = nxt + ([items[-1]] if len(items) % 2 else [])
        v8, r8 = items[0][0], items[0][1]
        mx = jnp.max(v8, axis=0, keepdims=True)
        pos = jnp.min(jnp.where(v8 == mx, r8, n_rows), axis=0, keepdims=True)
        top_v.append(mx)
        top_i.append(pos if payload is None else jnp.max(jnp.where(r8 == pos, items[0][2], -1), axis=0, keepdims=True))
        vals = jnp.where(row == pos, NEG_INF, vals)
    return jnp.concatenate(top_v, axis=0), jnp.concatenate(top_i, axis=0)


def _peer_subkey_topk_kernel(q_ref, sk_ref, sv_ref, si_ref):
    s_t = _nt_dot(sk_ref[...], q_ref[...])
    sv, si = _topk_rows(s_t, PEER_NKEYS)
    sv_ref[...] = sv
    si_ref[...] = si


def _peer_expert_topk_kernel(sv0_ref, si0_ref, sv1_ref, si1_ref, idx_ref, gate_ref):
    sv0, si0, sv1, si1 = sv0_ref[...], si0_ref[...], sv1_ref[...], si1_ref[...]
    k = PEER_TOPK
    row8 = lax.broadcasted_iota(jnp.int32, (SUBLANES, sv0.shape[1]), 0)
    cand = [sv0[0:1] + sv1]
    cidx = [si0[0:1] * PEER_NKEYS + si1]
    for i in range(1, SUBLANES):
        cand.append(jnp.where(row8 < k // (i + 1), sv0[i:i + 1] + sv1[0:SUBLANES], NEG_INF))
        cidx.append(si0[i:i + 1] * PEER_NKEYS + si1[0:SUBLANES])
    cand.append(sv0[SUBLANES:k] + sv1[0:1])
    cidx.append(si0[SUBLANES:k] * PEER_NKEYS + si1[0:1])
    cand = jnp.concatenate(cand, axis=0)
    cidx = jnp.concatenate(cidx, axis=0)
    top_s, eidx = _topk_rows(cand, cand.shape[0], payload=cidx)
    e = jnp.exp(top_s - jnp.max(top_s, axis=0, keepdims=True))
    idx_ref[...] = eidx
    gate_ref[...] = e / jnp.sum(e, axis=0, keepdims=True)


def _peer_route(qry, sub_keys, tm):
    t = qry.shape[0]
    tm = min(tm, t)
    nhp = 2 * PEER_HEADS
    half = pl.BlockSpec((None, PEER_TOPK, tm), lambda i, j: (j, 0, i))
    sv, si = pl.pallas_call(
        _peer_subkey_topk_kernel,
        grid=(t // tm, nhp),
        in_specs=[pl.BlockSpec((tm, PEER_SUBK), lambda i, j: (i, j)),
                  pl.BlockSpec((None, PEER_NKEYS, PEER_SUBK), lambda i, j: (j, 0, 0))],
        out_specs=[half, half],
        out_shape=[jax.ShapeDtypeStruct((nhp, PEER_TOPK, t), F32),
                   jax.ShapeDtypeStruct((nhp, PEER_TOPK, t), jnp.int32)],
        compiler_params=_params("parallel", "arbitrary"),
        name="peer_subkey_topk",
    )(qry, sub_keys.reshape(nhp, PEER_NKEYS, PEER_SUBK).astype(BF16))
    first = pl.BlockSpec((None, PEER_TOPK, tm), lambda i, j: (2 * j, 0, i))
    second = pl.BlockSpec((None, PEER_TOPK, tm), lambda i, j: (2 * j + 1, 0, i))
    head = pl.BlockSpec((None, PEER_TOPK, tm), lambda i, j: (j, 0, i))
    eidx, gate = pl.pallas_call(
        _peer_expert_topk_kernel,
        grid=(t // tm, PEER_HEADS),
        in_specs=[first, first, second, second],
        out_specs=[head, head],
        out_shape=[jax.ShapeDtypeStruct((PEER_HEADS, PEER_TOPK, t), jnp.int32),
                   jax.ShapeDtypeStruct((PEER_HEADS, PEER_TOPK, t), F32)],
        compiler_params=_params("parallel", "arbitrary"),
        name="peer_expert_topk",
    )(sv, si, sv, si)
    n_sel = PEER_HEADS * PEER_TOPK
    return eidx.reshape(n_sel, t).T, gate.reshape(n_sel, t).T


N_SEL = PEER_HEADS * PEER_TOPK
GATHER_TOKENS = 8
GATHER_AHEAD = 2
GATHER_SLOTS = GATHER_AHEAD + 1
GATHER_PIPE = 2


def _gelu(x):
    return 0.5 * x * (1.0 + lax.erf(x * (2.0 ** -0.5)))


def _peer_gather_kernel(idx_ref, idx_next_ref, idx_ahead_ref, x_ref, gate_ref, uv_hbm, f_ref,
                        buf_ring, sem, p_ref, w_ref):
    tb = x_ref.shape[0]
    rows = x_ref.shape[1]
    i = pl.program_id(0)
    slot = i % GATHER_SLOTS

    def start_row(ids_ref, dst_slot, t, k):
        pltpu.make_async_copy(uv_hbm.at[ids_ref[t, k]], buf_ring.at[dst_slot, k, t],
                              sem.at[dst_slot]).start(priority=k % 2)

    def wait_slot(s):
        pltpu.make_async_copy(buf_ring.at[s], buf_ring.at[s], sem.at[s]).wait()

    @pl.when(i == 0)
    def _():
        for b, ids_ref in enumerate((idx_ref, idx_next_ref)):
            def per_token(t, carry, b=b, ids_ref=ids_ref):
                for k in range(N_SEL):
                    start_row(ids_ref, b, t, k)
                return carry
            lax.fori_loop(0, tb, per_token, 0)

    half = N_SEL // 2

    def evaluate(buf, ahead_slot):
        def per_token(t, carry):
            x = x_ref[t].astype(F32)
            for k in range(N_SEL):
                if k % 2 == 0:
                    start_row(idx_ahead_ref, ahead_slot, t, k // 2)
                p = buf[k, t, 0:rows, :].astype(F32) * x
                part = p[0:SUBLANES]
                for r in range(1, rows // SUBLANES):
                    part = part + p[r * SUBLANES:(r + 1) * SUBLANES]
                p_ref[pl.ds(k * SUBLANES, SUBLANES), :] = part
            s = p_ref[pl.ds(0, N_SEL, stride=SUBLANES), :]
            for r in range(1, SUBLANES):
                s = s + p_ref[pl.ds(r, N_SEL, stride=SUBLANES), :]
            act = jnp.sum(s, axis=-1, keepdims=True)
            gate_col = jnp.broadcast_to(gate_ref[pl.ds(t, 1), :], (N_SEL, N_SEL)).T
            w_ref[...] = gate_col * _gelu(act)
            acc = jnp.zeros((rows, LANES), F32)
            for k in range(N_SEL):
                if k % 2 == 0:
                    start_row(idx_ahead_ref, ahead_slot, t, half + k // 2)
                acc = acc + w_ref[pl.ds(k, 1), :] * buf[k, t, rows:2 * rows, :].astype(F32)
            f_ref[t] = acc
            return carry

        lax.fori_loop(0, tb, per_token, 0)

    wait_slot(slot)
    evaluate(buf_ring.at[slot], (i + GATHER_AHEAD) % GATHER_SLOTS)

    @pl.when(i == pl.num_programs(0) - 1)
    def _():
        for a in range(1, GATHER_AHEAD + 1):
            wait_slot((i + a) % GATHER_SLOTS)


def _peer_experts(x_bf, eidx, gate, uv):
    t, d = x_bf.shape
    rows = d // LANES
    tb = GATHER_TOKENS
    assert t % tb == 0
    xs = x_bf.reshape(t, rows, LANES)
    nblk = t // tb
    idx_spec = lambda ahead: pl.BlockSpec((tb, N_SEL), lambda i: (jnp.minimum(i + ahead, nblk - 1), 0),
                                          memory_space=pltpu.SMEM)
    f = pl.pallas_call(
        _peer_gather_kernel,
        grid=(nblk,),
        in_specs=[idx_spec(0), idx_spec(1), idx_spec(GATHER_AHEAD),
                  pl.BlockSpec((tb, rows, LANES), lambda i: (i, 0, 0)),
                  pl.BlockSpec((tb, N_SEL), lambda i: (i, 0)),
                  pl.BlockSpec(memory_space=pl.ANY)],
        out_specs=pl.BlockSpec((tb, rows, LANES), lambda i: (i, 0, 0)),
        out_shape=jax.ShapeDtypeStruct((t, rows, LANES), F32),
        scratch_shapes=[pltpu.VMEM((GATHER_SLOTS, N_SEL, tb, 2 * rows, LANES), BF16),
                        pltpu.SemaphoreType.DMA((GATHER_SLOTS,)),
                        pltpu.VMEM((N_SEL * SUBLANES, LANES), F32),
                        pltpu.VMEM((N_SEL, LANES), F32)],
        compiler_params=_params("arbitrary"),
        name="peer_experts",
    )(eidx, eidx, eidx, xs, gate, uv)
    return f.reshape(t, d)


def _peer_ffn(h, h_bf, w_query, sub_keys, u_tab, v_tab, ln_g, ln_b):
    t, d = h.shape
    rows = d // LANES
    n_exp = u_tab.shape[0]
    qry = _matmul(h_bf, w_query.astype(BF16), BF16, 1024, 512)
    eidx, gate = _peer_route(qry, sub_keys, 512)
    uv = jnp.concatenate([u_tab.reshape(n_exp, rows, LANES), v_tab.reshape(n_exp, rows, LANES)],
                         axis=1).astype(BF16)
    f = _peer_experts(h_bf, eidx, gate, uv)
    return _residual_ln(h, f, ln_g, ln_b, 256)


def _rope_tables(positions):
    half = DIFF_QK_DIM // 2
    inv = jnp.power(ROPE_THETA, -jnp.arange(half, dtype=F32) * (2.0 / DIFF_QK_DIM))
    ang = positions.astype(F32).reshape(-1, 1) * inv
    cos, sin = jnp.cos(ang), jnp.sin(ang)
    return (jnp.concatenate([cos, cos, cos, cos], axis=-1),
            jnp.concatenate([-sin, sin, -sin, sin], axis=-1))


def kernel(x, mem, positions, attn_w_in, attn_lambda, attn_head_norm, mlstm_w_in, mlstm_gate_bias, mlstm_head_norm, mem_w_kv, w_out, ln_mix_g, ln_mix_b, peer_w_query, peer_sub_keys, peer_u, peer_v, ln_ffn_g, ln_ffn_b):
    batch, seq, d = x.shape
    n_mem = mem.shape[1]
    t = batch * seq
    mem_kv = _matmul(mem.reshape(batch * n_mem, d).astype(BF16), mem_w_kv.astype(BF16), BF16, 1024, 512)
    cos_t, sin_t = _rope_tables(positions)
    h = x.reshape(t, d)
    h_bf = h.astype(BF16)
    for i in range(DEPTH):
        j = i // 2
        if i % 2 == 0:
            proj = _matmul(h_bf, attn_w_in[j].astype(BF16), BF16, 1024, 512)
            qk = _rope(proj, cos_t, sin_t, 512)
            lam_init = 0.8 - 0.6 * math.exp(-0.3 * i)
            mix = _diff_attention(qk, proj, attn_lambda[j], attn_head_norm[j], batch, seq, lam_init, 512)
            qm_block0 = 3 * MIX_HEADS
        else:
            n_main = 4 * MIX_WIDTH + MEM_WIDTH
            w_in = mlstm_w_in[j]
            proj = _matmul(h_bf, w_in[:, :n_main].astype(BF16), BF16, 1024, 512)
            n_gates = 4 * MIX_HEADS
            w_gates = jnp.pad(w_in[:, n_main:], ((0, 0), (0, LANES - n_gates))).astype(BF16)
            gates = _matmul(h_bf, w_gates, F32, 1024, LANES)[:, :n_gates]
            gates_t = gates.reshape(batch, seq // MLSTM_CHUNK, MLSTM_CHUNK, 4, MIX_HEADS).transpose(0, 3, 4, 1, 2)
            mix = _mlstm(proj, gates_t, mlstm_gate_bias[j], mlstm_head_norm[j], batch, seq)
            qm_block0 = 4 * MIX_HEADS
        mem_out = _mem_attention(proj, qm_block0, mem_kv, batch, seq, n_mem, 512)
        h, h_bf = _outproj_ln(mix, mem_out, h, w_out[i], ln_mix_g[i], ln_mix_b[i], 256)
        h, h_bf = _peer_ffn(h, h_bf, peer_w_query[i], peer_sub_keys[i], peer_u[i], peer_v[i],
                            ln_ffn_g[i], ln_ffn_b[i])
    return h.reshape(batch, seq, d)
```

```python
import functools
import math

import jax
import jax.numpy as jnp
from jax import lax
from jax.experimental import pallas as pl
from jax.experimental.pallas import tpu as pltpu

F32 = jnp.float32
BF16 = jnp.bfloat16

HEAD_DIM = 128
MIX_HEADS = 12
MIX_WIDTH = MIX_HEADS * HEAD_DIM
MEM_HEADS = 4
MEM_WIDTH = MEM_HEADS * HEAD_DIM
DIFF_QK_DIM = HEAD_DIM // 2
ROPE_THETA = 10000.0
MLSTM_CHUNK = 128
PEER_HEADS = 8
PEER_NKEYS = 128
PEER_TOPK = 16
PEER_SUBK = 128
DEPTH = 2
DN_ALPHA = (2.0 * DEPTH) ** 0.25
LN_EPS = 1e-5

LANES = 128
SUBLANES = 8
VMEM_LIMIT_BYTES = 48 * 1024 * 1024
NEG_INF = float("-inf")
LOG2_E = math.log2(math.e)
ATTN_SUB_ROWS = 128
MLSTM_UNROLL = 4


def _params(*sem):
    return pltpu.CompilerParams(dimension_semantics=sem, vmem_limit_bytes=VMEM_LIMIT_BYTES)


def _nt_dot(a, b):
    return lax.dot_general(a, b, (((1,), (1,)), ((), ())), preferred_element_type=F32)


def _matmul_kernel(a_ref, b_ref, o_ref, b_bf_ref):
    @pl.when(pl.program_id(1) == 0)
    def _():
        b_bf_ref[...] = b_ref[...].astype(BF16)

    o_ref[...] = jnp.dot(a_ref[...], b_bf_ref[...], preferred_element_type=F32).astype(o_ref.dtype)


def _matmul(a, b, out_dtype, tm, tn, n=None):
    m, k = a.shape
    n = b.shape[1] if n is None else n
    tm, tn = min(tm, m), min(tn, n)
    assert m % tm == 0 and n % tn == 0
    return pl.pallas_call(
        _matmul_kernel,
        grid=(n // tn, m // tm),
        in_specs=[pl.BlockSpec((tm, k), lambda j, i: (i, 0)),
                  pl.BlockSpec((k, tn), lambda j, i: (0, j))],
        out_specs=pl.BlockSpec((tm, tn), lambda j, i: (i, j)),
        out_shape=jax.ShapeDtypeStruct((m, n), out_dtype),
        scratch_shapes=[pltpu.VMEM((k, tn), BF16)],
        compiler_params=_params("arbitrary", "arbitrary"),
        name="matmul",
    )(a, b)


def _rope_kernel(x_ref, cos_ref, sin_ref, o_ref):
    cos, sin = cos_ref[...], sin_ref[...]
    lane = lax.broadcasted_iota(jnp.int32, cos.shape, 1)
    first = (lane % DIFF_QK_DIM) < (DIFF_QK_DIM // 2)
    scale = jnp.where(pl.program_id(1) == 0, DIFF_QK_DIM ** -0.5 * LOG2_E, 1.0).astype(F32)
    for h in range(MIX_HEADS):
        cols = slice(h * HEAD_DIM, (h + 1) * HEAD_DIM)
        x = x_ref[:, cols].astype(F32)
        rot = jnp.where(first, pltpu.roll(x, LANES - DIFF_QK_DIM // 2, 1), pltpu.roll(x, DIFF_QK_DIM // 2, 1))
        o_ref[:, cols] = ((x * cos + rot * sin) * scale).astype(o_ref.dtype)


def _rope(proj, cos_t, sin_t, tm):
    t = proj.shape[0]
    tm = min(tm, t)
    blk = pl.BlockSpec((tm, MIX_WIDTH), lambda i, j: (i, j))
    tab = pl.BlockSpec((tm, LANES), lambda i, j: (i, 0))
    return pl.pallas_call(
        _rope_kernel,
        grid=(t // tm, 2),
        in_specs=[blk, tab, tab],
        out_specs=blk,
        out_shape=jax.ShapeDtypeStruct((t, 2 * MIX_WIDTH), BF16),
        compiler_params=_params("parallel", "arbitrary"),
        name="rope",
    )(proj, cos_t, sin_t)


def _diff_attn_kernel(lam_ref, q_ref, k_ref, v_ref, g_ref, o_ref, vext_ref, *, lam_init):
    @pl.when(pl.program_id(2) == 0)
    def _():
        vext_ref[:, 0:HEAD_DIM] = v_ref[...]
        vext_ref[:, HEAD_DIM:2 * HEAD_DIM] = jnp.ones(v_ref.shape, v_ref.dtype)

    lp = lam_ref[...]
    lam = (jnp.exp(jnp.sum(lp[0:1] * lp[1:2], axis=-1, keepdims=True))
           - jnp.exp(jnp.sum(lp[2:3] * lp[3:4], axis=-1, keepdims=True)) + lam_init)
    sub = ATTN_SUB_ROWS
    for r0 in range(0, q_ref.shape[0], sub):
        q = q_ref[r0:r0 + sub, :]
        lane = lax.broadcasted_iota(jnp.int32, q.shape, 1)
        zero = jnp.zeros_like(q)
        qq = jnp.concatenate([jnp.where(lane < DIFF_QK_DIM, q, zero), jnp.where(lane >= DIFF_QK_DIM, q, zero)],
                             axis=0)
        s = _nt_dot(qq, k_ref[...])
        e = jnp.exp2(s - jnp.max(s, axis=-1, keepdims=True)).astype(BF16)
        pv = jnp.dot(e, vext_ref[...], preferred_element_type=F32)
        pv = pv[:, 0:HEAD_DIM] / pv[:, HEAD_DIM:HEAD_DIM + 1]
        out = pv[:sub] - lam * pv[sub:]
        out = out * lax.rsqrt(jnp.mean(out * out, axis=-1, keepdims=True) + LN_EPS) * g_ref[...] * (1.0 - lam_init)
        o_ref[r0:r0 + sub, :] = out.astype(o_ref.dtype)


def _diff_attention(qk, proj, lam_params, norm_g, batch, seq, lam_init, tq):
    tq = min(tq, seq)
    nq = seq // tq
    return pl.pallas_call(
        functools.partial(_diff_attn_kernel, lam_init=lam_init),
        grid=(batch, MIX_HEADS, nq),
        in_specs=[pl.BlockSpec((4, DIFF_QK_DIM), lambda b, h, i: (0, 0)),
                  pl.BlockSpec((tq, HEAD_DIM), lambda b, h, i: (b * nq + i, h)),
                  pl.BlockSpec((seq, HEAD_DIM), lambda b, h, i: (b, MIX_HEADS + h)),
                  pl.BlockSpec((seq, HEAD_DIM), lambda b, h, i: (b, 2 * MIX_HEADS + h)),
                  pl.BlockSpec((None, 1, HEAD_DIM), lambda b, h, i: (h, 0, 0))],
        out_specs=pl.BlockSpec((tq, HEAD_DIM), lambda b, h, i: (b * nq + i, h)),
        out_shape=jax.ShapeDtypeStruct((batch * seq, MIX_WIDTH), BF16),
        scratch_shapes=[pltpu.VMEM((seq, 2 * HEAD_DIM), BF16)],
        compiler_params=_params("arbitrary", "arbitrary", "arbitrary"),
        name="diff_attention",
    )(lam_params, qk, qk, proj, norm_g.reshape(MIX_HEADS, 1, HEAD_DIM))


def _mem_attn_kernel(q_ref, k_ref, v_ref, o_ref):
    s = _nt_dot(q_ref[...], k_ref[...]) * (HEAD_DIM ** -0.5)
    e = jnp.exp(s - jnp.max(s, axis=-1, keepdims=True))
    l = jnp.sum(e, axis=-1, keepdims=True)
    o_ref[...] = (jnp.dot(e.astype(BF16), v_ref[...], preferred_element_type=F32) / l).astype(o_ref.dtype)


def _mem_attention(proj, qm_block0, mem_kv, batch, seq, n_mem, tq):
    tq = min(tq, seq)
    nq = seq // tq
    return pl.pallas_call(
        _mem_attn_kernel,
        grid=(batch, MEM_HEADS, nq),
        in_specs=[pl.BlockSpec((tq, HEAD_DIM), lambda b, h, i: (b * nq + i, qm_block0 + h)),
                  pl.BlockSpec((n_mem, HEAD_DIM), lambda b, h, i: (b, h)),
                  pl.BlockSpec((n_mem, HEAD_DIM), lambda b, h, i: (b, MEM_HEADS + h))],
        out_specs=pl.BlockSpec((tq, HEAD_DIM), lambda b, h, i: (b * nq + i, h)),
        out_shape=jax.ShapeDtypeStruct((batch * seq, MEM_WIDTH), BF16),
        compiler_params=_params("parallel", "arbitrary", "arbitrary"),
        name="mem_attention",
    )(proj, mem_kv, mem_kv)


def _layer_norm_rows(z, g, b):
    mu = jnp.mean(z, axis=-1, keepdims=True)
    zc = z - mu
    var = jnp.mean(zc * zc, axis=-1, keepdims=True)
    return zc * lax.rsqrt(var + LN_EPS) * g + b


def _outproj_ln_kernel(mix_ref, mem_ref, h_ref, w_mix_ref, w_mem_ref, g_ref, b_ref, o_ref, obf_ref):
    y = (jnp.dot(mix_ref[...], w_mix_ref[...], preferred_element_type=F32)
         + jnp.dot(mem_ref[...], w_mem_ref[...], preferred_element_type=F32))
    out = _layer_norm_rows(DN_ALPHA * h_ref[...] + y, g_ref[...], b_ref[...])
    o_ref[...] = out
    obf_ref[...] = out.astype(BF16)


def _outproj_ln(mix, mem_out, h, w_out, g, b, tm):
    t, d = h.shape
    tm = min(tm, t)
    row = lambda i: (i, 0)
    fixed = lambda i: (0, 0)
    return pl.pallas_call(
        _outproj_ln_kernel,
        grid=(t // tm,),
        in_specs=[pl.BlockSpec((tm, MIX_WIDTH), row), pl.BlockSpec((tm, MEM_WIDTH), row),
                  pl.BlockSpec((tm, d), row),
                  pl.BlockSpec((MIX_WIDTH, d), fixed), pl.BlockSpec((MEM_WIDTH, d), fixed),
                  pl.BlockSpec((1, d), fixed), pl.BlockSpec((1, d), fixed)],
        out_specs=[pl.BlockSpec((tm, d), row), pl.BlockSpec((tm, d), row)],
        out_shape=[jax.ShapeDtypeStruct((t, d), F32), jax.ShapeDtypeStruct((t, d), BF16)],
        compiler_params=_params("parallel"),
        name="outproj_ln",
    )(mix, mem_out, h, w_out[:MIX_WIDTH].astype(BF16), w_out[MIX_WIDTH:].astype(BF16),
      g.reshape(1, d), b.reshape(1, d))


def _residual_ln_kernel(h_ref, f_ref, g_ref, b_ref, o_ref, obf_ref):
    out = _layer_norm_rows(DN_ALPHA * h_ref[...] + f_ref[...], g_ref[...], b_ref[...])
    o_ref[...] = out
    obf_ref[...] = out.astype(BF16)


def _residual_ln(h, f, g, b, tm):
    t, d = h.shape
    tm = min(tm, t)
    row = lambda i: (i, 0)
    fixed = lambda i: (0, 0)
    return pl.pallas_call(
        _residual_ln_kernel,
        grid=(t // tm,),
        in_specs=[pl.BlockSpec((tm, d), row), pl.BlockSpec((tm, d), row),
                  pl.BlockSpec((1, d), fixed), pl.BlockSpec((1, d), fixed)],
        out_specs=[pl.BlockSpec((tm, d), row), pl.BlockSpec((tm, d), row)],
        out_shape=[jax.ShapeDtypeStruct((t, d), F32), jax.ShapeDtypeStruct((t, d), BF16)],
        compiler_params=_params("parallel"),
        name="residual_ln",
    )(h, f, g.reshape(1, d), b.reshape(1, d))


def _lane_cumsum(x, reverse):
    lane = lax.broadcasted_iota(jnp.int32, x.shape, 1)
    sh = 1
    while sh < LANES:
        if reverse:
            x = x + jnp.where(lane < LANES - sh, pltpu.roll(x, LANES - sh, 1), 0.0)
        else:
            x = x + jnp.where(lane >= sh, pltpu.roll(x, sh, 1), 0.0)
        sh *= 2
    return x


def _log_sigmoid(x):
    return jnp.minimum(x, 0.0) - jnp.log1p(jnp.exp(-jnp.abs(x)))


def _mlstm_kernel(bias_ref, q_ref, k_ref, v_ref, o_ref, g_ref, ng_ref, out_ref,
                  brow_ref, ig_ref, blast_ref, amax_ref, ms_ref, inner_ref, mi_ref, bcol_ref, kv_ref, cs_ref,
                  state_ref, m_ref):
    L = MLSTM_CHUNK
    D = HEAD_DIM
    scale = HEAD_DIM ** -0.5
    hd = pl.program_id(1)
    nc = q_ref.shape[0] // L
    jj = lax.broadcasted_iota(jnp.int32, (L, L), 0)
    ss = lax.broadcasted_iota(jnp.int32, (L, L), 1)
    masks = (ss <= jj, ss >= jj)
    ones_blk = jnp.ones((L, D), BF16)

    for d in range(2):
        lf = _log_sigmoid(g_ref[2 * d + 1] + bias_ref[2 * d + 1, hd])
        brow_ref[d] = _lane_cumsum(lf, reverse=(d == 1))
        blast_ref[d] = jnp.broadcast_to(jnp.sum(lf, axis=-1, keepdims=True), (nc, L))
        ig_ref[d] = g_ref[2 * d] + bias_ref[2 * d, hd]

    def pass1(c, carry):
        r0 = pl.multiple_of(c * L, L)
        rows = pl.ds(r0, L)
        q = q_ref[rows, :]
        k = k_ref[rows, :]
        v = v_ref[rows, :]
        kt = k.astype(F32).T.astype(BF16)
        qk = _nt_dot(q, k) * scale
        for d in range(2):
            b_row = jnp.broadcast_to(brow_ref[d, pl.ds(c, 1), :], (L, L))
            ig_row = jnp.broadcast_to(ig_ref[d, pl.ds(c, 1), :], (L, L))
            b_col = b_row.T
            ig_col = ig_row.T
            dm = jnp.where(masks[d], b_col - b_row + ig_row, NEG_INF)
            m_intra = jnp.max(dm, axis=-1, keepdims=True)
            s = (qk * jnp.exp(dm - m_intra)).astype(BF16)
            inner_ref[d, rows, :] = jnp.dot(s, jnp.concatenate([v, ones_blk], axis=1), preferred_element_type=F32)
            mi_ref[d, rows, :] = jnp.broadcast_to(m_intra, (L, D))
            bcol_ref[d, rows, :] = b_col
            a = blast_ref[d, pl.ds(c, 1), :] - b_col + ig_col
            a_max = jnp.max(a, axis=0, keepdims=True)
            wa = jnp.exp(a - a_max)
            vw = jnp.concatenate([v.astype(F32) * wa, wa], axis=1).astype(BF16)
            kv_ref[d, c] = jnp.dot(kt, vw, preferred_element_type=F32) * scale
            amax_ref[d, pl.ds(c, 1), :] = a_max
        return carry

    unroll = math.gcd(nc, MLSTM_UNROLL)

    def unrolled(body):
        def group(g, carry):
            for u in range(unroll):
                carry = body(g * unroll + u, carry)
            return carry
        return group

    lax.fori_loop(0, nc // unroll, unrolled(pass1), 0)

    state_ref[...] = jnp.zeros_like(state_ref)
    m_ref[...] = jnp.zeros_like(m_ref)

    def pass2(step, carry):
        for d in range(2):
            c = step if d == 0 else nc - 1 - step
            st = state_ref[d]
            m_old = m_ref[d]
            cs_ref[d, c] = st.astype(BF16)
            ms_ref[d, pl.ds(c, 1), :] = m_old
            b_last = blast_ref[d, pl.ds(c, 1), :]
            a_max = amax_ref[d, pl.ds(c, 1), :]
            m_new = jnp.maximum(b_last + m_old, a_max)
            decay = jnp.exp(b_last + m_old - m_new)
            w = jnp.exp(a_max - m_new)
            wide = lambda x: jnp.concatenate([x, x], axis=1)
            state_ref[d] = wide(decay) * st + wide(w) * kv_ref[d, c]
            m_ref[d] = m_new
        return carry

    lax.fori_loop(0, nc, pass2, 0)

    def pass3(c, carry):
        r0 = pl.multiple_of(c * L, L)
        rows = pl.ds(r0, L)
        q = q_ref[rows, :]
        hs = jnp.zeros((L, D), F32)
        for d in range(2):
            qc = jnp.dot(q, cs_ref[d, c], preferred_element_type=F32)
            inter = bcol_ref[d, rows, :] + ms_ref[d, pl.ds(c, 1), :]
            mi = mi_ref[d, rows, :]
            m_j = jnp.maximum(inter, mi)
            fi = jnp.exp(mi - m_j)
            wi = jnp.exp(inter - m_j)
            inn = inner_ref[d, rows, :]
            num = wi * qc[:, 0:D] + fi * inn[:, 0:D]
            den = wi * qc[:, D:2 * D] + fi * inn[:, D:2 * D]
            hs = hs + num / jnp.maximum(jnp.abs(den), jnp.exp(-m_j))
        hs = hs * lax.rsqrt(jnp.mean(hs * hs, axis=-1, keepdims=True) + LN_EPS) * ng_ref[...]
        out_ref[rows, :] = (jax.nn.sigmoid(o_ref[rows, :].astype(F32)) * hs).astype(out_ref.dtype)
        return carry

    lax.fori_loop(0, nc // unroll, unrolled(pass3), 0)


def _mlstm(proj, gates_t, gate_bias, norm_g, batch, seq):
    L, D = MLSTM_CHUNK, HEAD_DIM
    nc = seq // L
    col = lambda off: pl.BlockSpec((seq, D), lambda b, h: (b, off + h))
    st = lambda shape, dtype=F32: pltpu.VMEM(shape, dtype)
    return pl.pallas_call(
        _mlstm_kernel,
        grid=(batch, MIX_HEADS),
        in_specs=[pl.BlockSpec(memory_space=pltpu.SMEM),
                  col(0), col(MIX_HEADS), col(2 * MIX_HEADS), col(3 * MIX_HEADS),
                  pl.BlockSpec((None, 4, None, nc, L), lambda b, h: (b, 0, h, 0, 0)),
                  pl.BlockSpec((None, 1, D), lambda b, h: (h, 0, 0))],
        out_specs=pl.BlockSpec((seq, D), lambda b, h: (b, h)),
        out_shape=jax.ShapeDtypeStruct((batch * seq, MIX_WIDTH), BF16),
        scratch_shapes=[st((2, nc, L)), st((2, nc, L)), st((2, nc, L)), st((2, nc, L)), st((2, nc, L)),
                        st((2, seq, 2 * D)), st((2, seq, D)), st((2, seq, D)),
                        st((2, nc, D, 2 * D)), st((2, nc, D, 2 * D), BF16),
                        st((2, D, 2 * D)), st((2, 1, L))],
        compiler_params=_params("parallel", "arbitrary"),
        name="mlstm",
    )(gate_bias, proj, proj, proj, proj, gates_t, norm_g.reshape(MIX_HEADS, 1, HEAD_DIM))


def _topk_rows(vals, n_rows, payload=None):
    tm = vals.shape[1]
    row = lax.broadcasted_iota(jnp.int32, vals.shape, 0)
    row8 = lax.broadcasted_iota(jnp.int32, (SUBLANES, tm), 0)

    def better(a, b):
        keep = a[0] >= b[0]
        return tuple(jnp.where(keep, x, y) for x, y in zip(a, b))

    top_v, top_i = [], []
    for _ in range(PEER_TOPK):
        items = []
        for g in range(n_rows // SUBLANES):
            sl = slice(g * SUBLANES, (g + 1) * SUBLANES)
            items.append((vals[sl], row8 + g * SUBLANES) + (() if payload is None else (payload[sl],)))
        while len(items) > 1:
            nxt = [better(items[j], items[j + 1]) for j in range(0, len(items) - 1, 2)]
            items = nxt + ([items[-1]] if len(items) % 2 else [])
        v8, r8 = items[0][0], items[0][1]
        mx = jnp.max(v8, axis=0, keepdims=True)
        pos = jnp.min(jnp.where(v8 == mx, r8, n_rows), axis=0, keepdims=True)
        top_v.append(mx)
        top_i.append(pos if payload is None else jnp.max(jnp.where(r8 == pos, items[0][2], -1), axis=0, keepdims=True))
        vals = jnp.where(row == pos, NEG_INF, vals)
    return jnp.concatenate(top_v, axis=0), jnp.concatenate(top_i, axis=0)


def _peer_subkey_topk_kernel(q_ref, sk_ref, sv_ref, si_ref):
    s_t = _nt_dot(sk_ref[...], q_ref[...])
    sv, si = _topk_rows(s_t, PEER_NKEYS)
    sv_ref[...] = sv
    si_ref[...] = si


def _peer_expert_topk_kernel(sv0_ref, si0_ref, sv1_ref, si1_ref, idx_ref, gate_ref):
    sv0, si0, sv1, si1 = sv0_ref[...], si0_ref[...], sv1_ref[...], si1_ref[...]
    k = PEER_TOPK
    row8 = lax.broadcasted_iota(jnp.int32, (SUBLANES, sv0.shape[1]), 0)
    cand = [sv0[0:1] + sv1]
    cidx = [si0[0:1] * PEER_NKEYS + si1]
    for i in range(1, SUBLANES):
        cand.append(jnp.where(row8 < k // (i + 1), sv0[i:i + 1] + sv1[0:SUBLANES], NEG_INF))
        cidx.append(si0[i:i + 1] * PEER_NKEYS + si1[0:SUBLANES])
    cand.append(sv0[SUBLANES:k] + sv1[0:1])
    cidx.append(si0[SUBLANES:k] * PEER_NKEYS + si1[0:1])
    cand = jnp.concatenate(cand, axis=0)
    cidx = jnp.concatenate(cidx, axis=0)
    top_s, eidx = _topk_rows(cand, cand.shape[0], payload=cidx)
    e = jnp.exp(top_s - jnp.max(top_s, axis=0, keepdims=True))
    idx_ref[...] = eidx
    gate_ref[...] = e / jnp.sum(e, axis=0, keepdims=True)


def _peer_route(qry, sub_keys, tm):
    t = qry.shape[0]
    tm = min(tm, t)
    nhp = 2 * PEER_HEADS
    half = pl.BlockSpec((None, PEER_TOPK, tm), lambda i, j: (j, 0, i))
    sv, si = pl.pallas_call(
        _peer_subkey_topk_kernel,
        grid=(t // tm, nhp),
        in_specs=[pl.BlockSpec((tm, PEER_SUBK), lambda i, j: (i, j)),
                  pl.BlockSpec((None, PEER_NKEYS, PEER_SUBK), lambda i, j: (j, 0, 0))],
        out_specs=[half, half],
        out_shape=[jax.ShapeDtypeStruct((nhp, PEER_TOPK, t), F32),
                   jax.ShapeDtypeStruct((nhp, PEER_TOPK, t), jnp.int32)],
        compiler_params=_params("parallel", "arbitrary"),
        name="peer_subkey_topk",
    )(qry, sub_keys.reshape(nhp, PEER_NKEYS, PEER_SUBK).astype(BF16))
    first = pl.BlockSpec((None, PEER_TOPK, tm), lambda i, j: (2 * j, 0, i))
    second = pl.BlockSpec((None, PEER_TOPK, tm), lambda i, j: (2 * j + 1, 0, i))
    head = pl.BlockSpec((None, PEER_TOPK, tm), lambda i, j: (j, 0, i))
    eidx, gate = pl.pallas_call(
        _peer_expert_topk_kernel,
        grid=(t // tm, PEER_HEADS),
        in_specs=[first, first, second, second],
        out_specs=[head, head],
        out_shape=[jax.ShapeDtypeStruct((PEER_HEADS, PEER_TOPK, t), jnp.int32),
                   jax.ShapeDtypeStruct((PEER_HEADS, PEER_TOPK, t), F32)],
        compiler_params=_params("parallel", "arbitrary"),
        name="peer_expert_topk",
    )(sv, si, sv, si)
    n_sel = PEER_HEADS * PEER_TOPK
    return eidx.reshape(n_sel, t).T, gate.reshape(n_sel, t).T


N_SEL = PEER_HEADS * PEER_TOPK
GATHER_TOKENS = 8
GATHER_AHEAD = 2
GATHER_SLOTS = GATHER_AHEAD + 1
GATHER_PIPE = 2


def _gelu(x):
    return 0.5 * x * (1.0 + lax.erf(x * (2.0 ** -0.5)))


def _peer_gather_kernel(idx_ref, idx_next_ref, idx_ahead_ref, x_ref, gate_ref, uv_hbm, f_ref,
                        buf_ring, sem, p_ref, w_ref):
    tb = x_ref.shape[0]
    rows = x_ref.shape[1]
    i = pl.program_id(0)
    slot = i % GATHER_SLOTS

    def start_row(ids_ref, dst_slot, t, k):
        pltpu.make_async_copy(uv_hbm.at[ids_ref[t, k]], buf_ring.at[dst_slot, k, t],
                              sem.at[dst_slot]).start(priority=k % 2)

    def wait_slot(s):
        pltpu.make_async_copy(buf_ring.at[s], buf_ring.at[s], sem.at[s]).wait()

    @pl.when(i == 0)
    def _():
        for b, ids_ref in enumerate((idx_ref, idx_next_ref)):
            def per_token(t, carry, b=b, ids_ref=ids_ref):
                for k in range(N_SEL):
                    start_row(ids_ref, b, t, k)
                return carry
            lax.fori_loop(0, tb, per_token, 0)

    half = N_SEL // 2

    def evaluate(buf, ahead_slot):
        def per_token(t, carry):
            x = x_ref[t].astype(F32)
            for k in range(N_SEL):
                if k % 2 == 0:
                    start_row(idx_ahead_ref, ahead_slot, t, k // 2)
                p = buf[k, t, 0:rows, :].astype(F32) * x
                part = p[0:SUBLANES]
                for r in range(1, rows // SUBLANES):
                    part = part + p[r * SUBLANES:(r + 1) * SUBLANES]
                p_ref[pl.ds(k * SUBLANES, SUBLANES), :] = part
            s = p_ref[pl.ds(0, N_SEL, stride=SUBLANES), :]
            for r in range(1, SUBLANES):
                s = s + p_ref[pl.ds(r, N_SEL, stride=SUBLANES), :]
            act = jnp.sum(s, axis=-1, keepdims=True)
            gate_col = jnp.broadcast_to(gate_ref[pl.ds(t, 1), :], (N_SEL, N_SEL)).T
            w_ref[...] = gate_col * _gelu(act)
            acc = jnp.zeros((rows, LANES), F32)
            for k in range(N_SEL):
                if k % 2 == 0:
                    start_row(idx_ahead_ref, ahead_slot, t, half + k // 2)
                acc = acc + w_ref[pl.ds(k, 1), :] * buf[k, t, rows:2 * rows, :].astype(F32)
            f_ref[t] = acc
            return carry

        lax.fori_loop(0, tb, per_token, 0)

    wait_slot(slot)
    evaluate(buf_ring.at[slot], (i + GATHER_AHEAD) % GATHER_SLOTS)

    @pl.when(i == pl.num_programs(0) - 1)
    def _():
        for a in range(1, GATHER_AHEAD + 1):
            wait_slot((i + a) % GATHER_SLOTS)


def _peer_experts(x_bf, eidx, gate, uv):
    t, d = x_bf.shape
    rows = d // LANES
    tb = GATHER_TOKENS
    assert t % tb == 0
    xs = x_bf.reshape(t, rows, LANES)
    nblk = t // tb
    idx_spec = lambda ahead: pl.BlockSpec((tb, N_SEL), lambda i: (jnp.minimum(i + ahead, nblk - 1), 0),
                                          memory_space=pltpu.SMEM)
    f = pl.pallas_call(
        _peer_gather_kernel,
        grid=(nblk,),
        in_specs=[idx_spec(0), idx_spec(1), idx_spec(GATHER_AHEAD),
                  pl.BlockSpec((tb, rows, LANES), lambda i: (i, 0, 0)),
                  pl.BlockSpec((tb, N_SEL), lambda i: (i, 0)),
                  pl.BlockSpec(memory_space=pl.ANY)],
        out_specs=pl.BlockSpec((tb, rows, LANES), lambda i: (i, 0, 0)),
        out_shape=jax.ShapeDtypeStruct((t, rows, LANES), F32),
        scratch_shapes=[pltpu.VMEM((GATHER_SLOTS, N_SEL, tb, 2 * rows, LANES), BF16),
                        pltpu.SemaphoreType.DMA((GATHER_SLOTS,)),
                        pltpu.VMEM((N_SEL * SUBLANES, LANES), F32),
                        pltpu.VMEM((N_SEL, LANES), F32)],
        compiler_params=_params("arbitrary"),
        name="peer_experts",
    )(eidx, eidx, eidx, xs, gate, uv)
    return f.reshape(t, d)


def _peer_ffn(h, h_bf, w_query, sub_keys, u_tab, v_tab, ln_g, ln_b):
    t, d = h.shape
    rows = d // LANES
    n_exp = u_tab.shape[0]
    qry = _matmul(h_bf, w_query, BF16, 1024, 512)
    eidx, gate = _peer_route(qry, sub_keys, 1024)
    uv = jnp.concatenate([u_tab.reshape(n_exp, rows, LANES), v_tab.reshape(n_exp, rows, LANES)],
                         axis=1).astype(BF16)
    f = _peer_experts(h_bf, eidx, gate, uv)
    return _residual_ln(h, f, ln_g, ln_b, 256)


def _rope_tables(positions):
    half = DIFF_QK_DIM // 2
    inv = jnp.power(ROPE_THETA, -jnp.arange(half, dtype=F32) * (2.0 / DIFF_QK_DIM))
    ang = positions.astype(F32).reshape(-1, 1) * inv
    cos, sin = jnp.cos(ang), jnp.sin(ang)
    return (jnp.concatenate([cos, cos, cos, cos], axis=-1),
            jnp.concatenate([-sin, sin, -sin, sin], axis=-1))


def kernel(x, mem, positions, attn_w_in, attn_lambda, attn_head_norm, mlstm_w_in, mlstm_gate_bias, mlstm_head_norm, mem_w_kv, w_out, ln_mix_g, ln_mix_b, peer_w_query, peer_sub_keys, peer_u, peer_v, ln_ffn_g, ln_ffn_b):
    batch, seq, d = x.shape
    n_mem = mem.shape[1]
    t = batch * seq
    mem_kv = _matmul(mem.reshape(batch * n_mem, d).astype(BF16), mem_w_kv, BF16, 1024, 512)
    cos_t, sin_t = _rope_tables(positions)
    h = x.reshape(t, d)
    h_bf = h.astype(BF16)
    for i in range(DEPTH):
        j = i // 2
        if i % 2 == 0:
            proj = _matmul(h_bf, attn_w_in[j], BF16, 1024, 512)
            qk = _rope(proj, cos_t, sin_t, 512)
            lam_init = 0.8 - 0.6 * math.exp(-0.3 * i)
            mix = _diff_attention(qk, proj, attn_lambda[j], attn_head_norm[j], batch, seq, lam_init, 512)
            qm_block0 = 3 * MIX_HEADS
        else:
            n_main = 4 * MIX_WIDTH + MEM_WIDTH
            w_in = mlstm_w_in[j]
            proj = _matmul(h_bf, w_in, BF16, 1024, 512, n=n_main)
            n_gates = 4 * MIX_HEADS
            w_gates = jnp.pad(w_in[:, n_main:], ((0, 0), (0, LANES - n_gates)))
            gates = _matmul(h_bf, w_gates, F32, 1024, LANES)[:, :n_gates]
            gates_t = gates.reshape(batch, seq // MLSTM_CHUNK, MLSTM_CHUNK, 4, MIX_HEADS).transpose(0, 3, 4, 1, 2)
            mix = _mlstm(proj, gates_t, mlstm_gate_bias[j], mlstm_head_norm[j], batch, seq)
            qm_block0 = 4 * MIX_HEADS
        mem_out = _mem_attention(proj, qm_block0, mem_kv, batch, seq, n_mem, 512)
        h, h_bf = _outproj_ln(mix, mem_out, h, w_out[i], ln_mix_g[i], ln_mix_b[i], 256)
        h, h_bf = _peer_ffn(h, h_bf, peer_w_query[i], peer_sub_keys[i], peer_u[i], peer_v[i],
                            ln_ffn_g[i], ln_ffn_b[i])
    return h.reshape(batch, seq, d)
```

```python
import functools
import math

import jax
import jax.numpy as jnp
from jax import lax
from jax.experimental import pallas as pl
from jax.experimental.pallas import tpu as pltpu

F32 = jnp.float32
BF16 = jnp.bfloat16

HEAD_DIM = 128
MIX_HEADS = 12
MIX_WIDTH = MIX_HEADS * HEAD_DIM
MEM_HEADS = 4
MEM_WIDTH = MEM_HEADS * HEAD_DIM
DIFF_QK_DIM = HEAD_DIM // 2
ROPE_THETA = 10000.0
MLSTM_CHUNK = 128
PEER_HEADS = 8
PEER_NKEYS = 128
PEER_TOPK = 16
PEER_SUBK = 128
DEPTH = 2
DN_ALPHA = (2.0 * DEPTH) ** 0.25
LN_EPS = 1e-5

LANES = 128
SUBLANES = 8
VMEM_LIMIT_BYTES = 48 * 1024 * 1024
NEG_INF = float("-inf")
LOG2_E = math.log2(math.e)
ATTN_SUB_ROWS = 128
MLSTM_UNROLL = 4


def _params(*sem):
    return pltpu.CompilerParams(dimension_semantics=sem, vmem_limit_bytes=VMEM_LIMIT_BYTES)


def _nt_dot(a, b):
    return lax.dot_general(a, b, (((1,), (1,)), ((), ())), preferred_element_type=F32)


def _matmul_kernel(a_ref, b_ref, o_ref, b_bf_ref):
    @pl.when(pl.program_id(1) == 0)
    def _():
        b_bf_ref[...] = b_ref[...].astype(BF16)

    o_ref[...] = jnp.dot(a_ref[...], b_bf_ref[...], preferred_element_type=F32).astype(o_ref.dtype)


def _matmul(a, b, out_dtype, tm, tn, n=None):
    m, k = a.shape
    n = b.shape[1] if n is None else n
    tm, tn = min(tm, m), min(tn, n)
    assert m % tm == 0 and n % tn == 0
    return pl.pallas_call(
        _matmul_kernel,
        grid=(n // tn, m // tm),
        in_specs=[pl.BlockSpec((tm, k), lambda j, i: (i, 0)),
                  pl.BlockSpec((k, tn), lambda j, i: (0, j))],
        out_specs=pl.BlockSpec((tm, tn), lambda j, i: (i, j)),
        out_shape=jax.ShapeDtypeStruct((m, n), out_dtype),
        scratch_shapes=[pltpu.VMEM((k, tn), BF16)],
        compiler_params=_params("arbitrary", "arbitrary"),
        name="matmul",
    )(a, b)


def _rope_kernel(x_ref, cos_ref, sin_ref, o_ref):
    cos, sin = cos_ref[...], sin_ref[...]
    lane = lax.broadcasted_iota(jnp.int32, cos.shape, 1)
    first = (lane % DIFF_QK_DIM) < (DIFF_QK_DIM // 2)
    scale = jnp.where(pl.program_id(1) == 0, DIFF_QK_DIM ** -0.5 * LOG2_E, 1.0).astype(F32)
    for h in range(MIX_HEADS):
        cols = slice(h * HEAD_DIM, (h + 1) * HEAD_DIM)
        x = x_ref[:, cols].astype(F32)
        rot = jnp.where(first, pltpu.roll(x, LANES - DIFF_QK_DIM // 2, 1), pltpu.roll(x, DIFF_QK_DIM // 2, 1))
        o_ref[:, cols] = ((x * cos + rot * sin) * scale).astype(o_ref.dtype)


def _rope(proj, cos_t, sin_t, tm):
    t = proj.shape[0]
    tm = min(tm, t)
    blk = pl.BlockSpec((tm, MIX_WIDTH), lambda i, j: (i, j))
    tab = pl.BlockSpec((tm, LANES), lambda i, j: (i, 0))
    return pl.pallas_call(
        _rope_kernel,
        grid=(t // tm, 2),
        in_specs=[blk, tab, tab],
        out_specs=blk,
        out_shape=jax.ShapeDtypeStruct((t, 2 * MIX_WIDTH), BF16),
        compiler_params=_params("parallel", "arbitrary"),
        name="rope",
    )(proj, cos_t, sin_t)


def _diff_attn_kernel(lam_ref, q_ref, k_ref, v_ref, g_ref, o_ref, vext_ref, *, lam_init):
    @pl.when(pl.program_id(2) == 0)
    def _():
        vext_ref[:, 0:HEAD_DIM] = v_ref[...]
        vext_ref[:, HEAD_DIM:2 * HEAD_DIM] = jnp.ones(v_ref.shape, v_ref.dtype)

    lp = lam_ref[...]
    lam = (jnp.exp(jnp.sum(lp[0:1] * lp[1:2], axis=-1, keepdims=True))
           - jnp.exp(jnp.sum(lp[2:3] * lp[3:4], axis=-1, keepdims=True)) + lam_init)
    sub = ATTN_SUB_ROWS
    for r0 in range(0, q_ref.shape[0], sub):
        q = q_ref[r0:r0 + sub, :]
        lane = lax.broadcasted_iota(jnp.int32, q.shape, 1)
        zero = jnp.zeros_like(q)
        qq = jnp.concatenate([jnp.where(lane < DIFF_QK_DIM, q, zero), jnp.where(lane >= DIFF_QK_DIM, q, zero)],
                             axis=0)
        s = _nt_dot(qq, k_ref[...])
        e = jnp.exp2(s - jnp.max(s, axis=-1, keepdims=True)).astype(BF16)
        pv = jnp.dot(e, vext_ref[...], preferred_element_type=F32)
        pv = pv[:, 0:HEAD_DIM] / pv[:, HEAD_DIM:HEAD_DIM + 1]
        out = pv[:sub] - lam * pv[sub:]
        out = out * lax.rsqrt(jnp.mean(out * out, axis=-1, keepdims=True) + LN_EPS) * g_ref[...] * (1.0 - lam_init)
        o_ref[r0:r0 + sub, :] = out.astype(o_ref.dtype)


def _diff_attention(qk, proj, lam_params, norm_g, batch, seq, lam_init, tq):
    tq = min(tq, seq)
    nq = seq // tq
    return pl.pallas_call(
        functools.partial(_diff_attn_kernel, lam_init=lam_init),
        grid=(batch, MIX_HEADS, nq),
        in_specs=[pl.BlockSpec((4, DIFF_QK_DIM), lambda b, h, i: (0, 0)),
                  pl.BlockSpec((tq, HEAD_DIM), lambda b, h, i: (b * nq + i, h)),
                  pl.BlockSpec((seq, HEAD_DIM), lambda b, h, i: (b, MIX_HEADS + h)),
                  pl.BlockSpec((seq, HEAD_DIM), lambda b, h, i: (b, 2 * MIX_HEADS + h)),
                  pl.BlockSpec((None, 1, HEAD_DIM), lambda b, h, i: (h, 0, 0))],
        out_specs=pl.BlockSpec((tq, HEAD_DIM), lambda b, h, i: (b * nq + i, h)),
        out_shape=jax.ShapeDtypeStruct((batch * seq, MIX_WIDTH), BF16),
        scratch_shapes=[pltpu.VMEM((seq, 2 * HEAD_DIM), BF16)],
        compiler_params=_params("arbitrary", "arbitrary", "arbitrary"),
        name="diff_attention",
    )(lam_params, qk, qk, proj, norm_g.reshape(MIX_HEADS, 1, HEAD_DIM))


def _mem_attn_kernel(q_ref, k_ref, v_ref, o_ref):
    s = _nt_dot(q_ref[...], k_ref[...]) * (HEAD_DIM ** -0.5)
    e = jnp.exp(s - jnp.max(s, axis=-1, keepdims=True))
    l = jnp.sum(e, axis=-1, keepdims=True)
    o_ref[...] = (jnp.dot(e.astype(BF16), v_ref[...], preferred_element_type=F32) / l).astype(o_ref.dtype)


def _mem_attention(proj, qm_block0, mem_kv, batch, seq, n_mem, tq):
    tq = min(tq, seq)
    nq = seq // tq
    return pl.pallas_call(
        _mem_attn_kernel,
        grid=(batch, MEM_HEADS, nq),
        in_specs=[pl.BlockSpec((tq, HEAD_DIM), lambda b, h, i: (b * nq + i, qm_block0 + h)),
                  pl.BlockSpec((n_mem, HEAD_DIM), lambda b, h, i: (b, h)),
                  pl.BlockSpec((n_mem, HEAD_DIM), lambda b, h, i: (b, MEM_HEADS + h))],
        out_specs=pl.BlockSpec((tq, HEAD_DIM), lambda b, h, i: (b * nq + i, h)),
        out_shape=jax.ShapeDtypeStruct((batch * seq, MEM_WIDTH), BF16),
        compiler_params=_params("parallel", "arbitrary", "arbitrary"),
        name="mem_attention",
    )(proj, mem_kv, mem_kv)


def _layer_norm_rows(z, g, b):
    mu = jnp.mean(z, axis=-1, keepdims=True)
    zc = z - mu
    var = jnp.mean(zc * zc, axis=-1, keepdims=True)
    return zc * lax.rsqrt(var + LN_EPS) * g + b


def _outproj_ln_kernel(mix_ref, mem_ref, h_ref, w_mix_ref, w_mem_ref, g_ref, b_ref, o_ref, obf_ref):
    y = (jnp.dot(mix_ref[...], w_mix_ref[...], preferred_element_type=F32)
         + jnp.dot(mem_ref[...], w_mem_ref[...], preferred_element_type=F32))
    out = _layer_norm_rows(DN_ALPHA * h_ref[...] + y, g_ref[...], b_ref[...])
    o_ref[...] = out
    obf_ref[...] = out.astype(BF16)


def _outproj_ln(mix, mem_out, h, w_out, g, b, tm):
    t, d = h.shape
    tm = min(tm, t)
    row = lambda i: (i, 0)
    fixed = lambda i: (0, 0)
    return pl.pallas_call(
        _outproj_ln_kernel,
        grid=(t // tm,),
        in_specs=[pl.BlockSpec((tm, MIX_WIDTH), row), pl.BlockSpec((tm, MEM_WIDTH), row),
                  pl.BlockSpec((tm, d), row),
                  pl.BlockSpec((MIX_WIDTH, d), fixed), pl.BlockSpec((MEM_WIDTH, d), fixed),
                  pl.BlockSpec((1, d), fixed), pl.BlockSpec((1, d), fixed)],
        out_specs=[pl.BlockSpec((tm, d), row), pl.BlockSpec((tm, d), row)],
        out_shape=[jax.ShapeDtypeStruct((t, d), F32), jax.ShapeDtypeStruct((t, d), BF16)],
        compiler_params=_params("parallel"),
        name="outproj_ln",
    )(mix, mem_out, h, w_out[:MIX_WIDTH].astype(BF16), w_out[MIX_WIDTH:].astype(BF16),
      g.reshape(1, d), b.reshape(1, d))


def _residual_ln_kernel(h_ref, f_ref, g_ref, b_ref, o_ref, obf_ref):
    out = _layer_norm_rows(DN_ALPHA * h_ref[...] + f_ref[...], g_ref[...], b_ref[...])
    o_ref[...] = out
    obf_ref[...] = out.astype(BF16)


def _residual_ln(h, f, g, b, tm):
    t, d = h.shape
    tm = min(tm, t)
    row = lambda i: (i, 0)
    fixed = lambda i: (0, 0)
    return pl.pallas_call(
        _residual_ln_kernel,
        grid=(t // tm,),
        in_specs=[pl.BlockSpec((tm, d), row), pl.BlockSpec((tm, d), row),
                  pl.BlockSpec((1, d), fixed), pl.BlockSpec((1, d), fixed)],
        out_specs=[pl.BlockSpec((tm, d), row), pl.BlockSpec((tm, d), row)],
        out_shape=[jax.ShapeDtypeStruct((t, d), F32), jax.ShapeDtypeStruct((t, d), BF16)],
        compiler_params=_params("parallel"),
        name="residual_ln",
    )(h, f, g.reshape(1, d), b.reshape(1, d))


def _lane_cumsum(x, reverse):
    lane = lax.broadcasted_iota(jnp.int32, x.shape, 1)
    sh = 1
    while sh < LANES:
        if reverse:
            x = x + jnp.where(lane < LANES - sh, pltpu.roll(x, LANES - sh, 1), 0.0)
        else:
            x = x + jnp.where(lane >= sh, pltpu.roll(x, sh, 1), 0.0)
        sh *= 2
    return x


def _log_sigmoid(x):
    return jnp.minimum(x, 0.0) - jnp.log1p(jnp.exp(-jnp.abs(x)))


def _mlstm_kernel(bias_ref, q_ref, k_ref, v_ref, o_ref, g_ref, ng_ref, out_ref,
                  brow_ref, ig_ref, blast_ref, amax_ref, ms_ref, inner_ref, mi_ref, bcol_ref, kv_ref, cs_ref,
                  state_ref, m_ref):
    L = MLSTM_CHUNK
    D = HEAD_DIM
    scale = HEAD_DIM ** -0.5
    hd = pl.program_id(1)
    nc = q_ref.shape[0] // L
    jj = lax.broadcasted_iota(jnp.int32, (L, L), 0)
    ss = lax.broadcasted_iota(jnp.int32, (L, L), 1)
    masks = (ss <= jj, ss >= jj)
    ones_blk = jnp.ones((L, D), BF16)

    for d in range(2):
        lf = _log_sigmoid(g_ref[2 * d + 1] + bias_ref[2 * d + 1, hd])
        brow_ref[d] = _lane_cumsum(lf, reverse=(d == 1))
        blast_ref[d] = jnp.broadcast_to(jnp.sum(lf, axis=-1, keepdims=True), (nc, L))
        ig_ref[d] = g_ref[2 * d] + bias_ref[2 * d, hd]

    def pass1(c, carry):
        r0 = pl.multiple_of(c * L, L)
        rows = pl.ds(r0, L)
        q = q_ref[rows, :]
        k = k_ref[rows, :]
        v = v_ref[rows, :]
        kt = k.astype(F32).T.astype(BF16)
        qk = _nt_dot(q, k) * scale
        for d in range(2):
            b_row = jnp.broadcast_to(brow_ref[d, pl.ds(c, 1), :], (L, L))
            ig_row = jnp.broadcast_to(ig_ref[d, pl.ds(c, 1), :], (L, L))
            b_col = b_row.T
            ig_col = ig_row.T
            dm = jnp.where(masks[d], b_col - b_row + ig_row, NEG_INF)
            m_intra = jnp.max(dm, axis=-1, keepdims=True)
            s = (qk * jnp.exp(dm - m_intra)).astype(BF16)
            inner_ref[d, rows, :] = jnp.dot(s, jnp.concatenate([v, ones_blk], axis=1), preferred_element_type=F32)
            mi_ref[d, rows, :] = jnp.broadcast_to(m_intra, (L, D))
            bcol_ref[d, rows, :] = b_col
            a = blast_ref[d, pl.ds(c, 1), :] - b_col + ig_col
            a_max = jnp.max(a, axis=0, keepdims=True)
            wa = jnp.exp(a - a_max)
            vw = jnp.concatenate([v.astype(F32) * wa, wa], axis=1).astype(BF16)
            kv_ref[d, c] = jnp.dot(kt, vw, preferred_element_type=F32) * scale
            amax_ref[d, pl.ds(c, 1), :] = a_max
        return carry

    unroll = math.gcd(nc, MLSTM_UNROLL)

    def unrolled(body):
        def group(g, carry):
            for u in range(unroll):
                carry = body(g * unroll + u, carry)
            return carry
        return group

    lax.fori_loop(0, nc // unroll, unrolled(pass1), 0)

    state_ref[...] = jnp.zeros_like(state_ref)
    m_ref[...] = jnp.zeros_like(m_ref)

    def pass2(step, carry):
        for d in range(2):
            c = step if d == 0 else nc - 1 - step
            st = state_ref[d]
            m_old = m_ref[d]
            cs_ref[d, c] = st.astype(BF16)
            ms_ref[d, pl.ds(c, 1), :] = m_old
            b_last = blast_ref[d, pl.ds(c, 1), :]
            a_max = amax_ref[d, pl.ds(c, 1), :]
            m_new = jnp.maximum(b_last + m_old, a_max)
            decay = jnp.exp(b_last + m_old - m_new)
            w = jnp.exp(a_max - m_new)
            wide = lambda x: jnp.concatenate([x, x], axis=1)
            state_ref[d] = wide(decay) * st + wide(w) * kv_ref[d, c]
            m_ref[d] = m_new
        return carry

    lax.fori_loop(0, nc, pass2, 0)

    def pass3(c, carry):
        r0 = pl.multiple_of(c * L, L)
        rows = pl.ds(r0, L)
        q = q_ref[rows, :]
        hs = jnp.zeros((L, D), F32)
        for d in range(2):
            qc = jnp.dot(q, cs_ref[d, c], preferred_element_type=F32)
            inter = bcol_ref[d, rows, :] + ms_ref[d, pl.ds(c, 1), :]
            mi = mi_ref[d, rows, :]
            m_j = jnp.maximum(inter, mi)
            fi = jnp.exp(mi - m_j)
            wi = jnp.exp(inter - m_j)
            inn = inner_ref[d, rows, :]
            num = wi * qc[:, 0:D] + fi * inn[:, 0:D]
            den = wi * qc[:, D:2 * D] + fi * inn[:, D:2 * D]
            hs = hs + num / jnp.maximum(jnp.abs(den), jnp.exp(-m_j))
        hs = hs * lax.rsqrt(jnp.mean(hs * hs, axis=-1, keepdims=True) + LN_EPS) * ng_ref[...]
        out_ref[rows, :] = (jax.nn.sigmoid(o_ref[rows, :].astype(F32)) * hs).astype(out_ref.dtype)
        return carry

    lax.fori_loop(0, nc // unroll, unrolled(pass3), 0)


def _mlstm(proj, gates_t, gate_bias, norm_g, batch, seq):
    L, D = MLSTM_CHUNK, HEAD_DIM
    nc = seq // L
    col = lambda off: pl.BlockSpec((seq, D), lambda b, h: (b, off + h))
    st = lambda shape, dtype=F32: pltpu.VMEM(shape, dtype)
    return pl.pallas_call(
        _mlstm_kernel,
        grid=(batch, MIX_HEADS),
        in_specs=[pl.BlockSpec(memory_space=pltpu.SMEM),
                  col(0), col(MIX_HEADS), col(2 * MIX_HEADS), col(3 * MIX_HEADS),
                  pl.BlockSpec((None, 4, None, nc, L), lambda b, h: (b, 0, h, 0, 0)),
                  pl.BlockSpec((None, 1, D), lambda b, h: (h, 0, 0))],
        out_specs=pl.BlockSpec((seq, D), lambda b, h: (b, h)),
        out_shape=jax.ShapeDtypeStruct((batch * seq, MIX_WIDTH), BF16),
        scratch_shapes=[st((2, nc, L)), st((2, nc, L)), st((2, nc, L)), st((2, nc, L)), st((2, nc, L)),
                        st((2, seq, 2 * D)), st((2, seq, D)), st((2, seq, D)),
                        st((2, nc, D, 2 * D)), st((2, nc, D, 2 * D), BF16),
                        st((2, D, 2 * D)), st((2, 1, L))],
        compiler_params=_params("parallel", "arbitrary"),
        name="mlstm",
    )(gate_bias, proj, proj, proj, proj, gates_t, norm_g.reshape(MIX_HEADS, 1, HEAD_DIM))


def _topk_rows(vals, n_rows, payload=None):
    tm = vals.shape[1]
    row = lax.broadcasted_iota(jnp.int32, vals.shape, 0)
    row8 = lax.broadcasted_iota(jnp.int32, (SUBLANES, tm), 0)

    def better(a, b):
        keep = a[0] >= b[0]
        return tuple(jnp.where(keep, x, y) for x, y in zip(a, b))

    top_v, top_i = [], []
    for _ in range(PEER_TOPK):
        items = []
        for g in range(n_rows // SUBLANES):
            sl = slice(g * SUBLANES, (g + 1) * SUBLANES)
            items.append((vals[sl], row8 + g * SUBLANES) + (() if payload is None else (payload[sl],)))
        while len(items) > 1:
            nxt = [better(items[j], items[j + 1]) for j in range(0, len(items) - 1, 2)]
            items = nxt + ([items[-1]] if len(items) % 2 else [])
        v8, r8 = items[0][0], items[0][1]
        mx = jnp.max(v8, axis=0, keepdims=True)
        pos = jnp.min(jnp.where(v8 == mx, r8, n_rows), axis=0, keepdims=True)
        top_v.append(mx)
        top_i.append(pos if payload is None else jnp.max(jnp.where(r8 == pos, items[0][2], -1), axis=0, keepdims=True))
        vals = jnp.where(row == pos, NEG_INF, vals)
    return jnp.concatenate(top_v, axis=0), jnp.concatenate(top_i, axis=0)


def _peer_subkey_topk_kernel(q_ref, sk_ref, sv_ref, si_ref):
    s_t = _nt_dot(sk_ref[...], q_ref[...])
    sv, si = _topk_rows(s_t, PEER_NKEYS)
    sv_ref[...] = sv
    si_ref[...] = si


def _peer_expert_topk_kernel(sv0_ref, si0_ref, sv1_ref, si1_ref, idx_ref, gate_ref):
    sv0, si0, sv1, si1 = sv0_ref[...], si0_ref[...], sv1_ref[...], si1_ref[...]
    k = PEER_TOPK
    row8 = lax.broadcasted_iota(jnp.int32, (SUBLANES, sv0.shape[1]), 0)
    cand = [sv0[0:1] + sv1]
    cidx = [si0[0:1] * PEER_NKEYS + si1]
    for i in range(1, SUBLANES):
        cand.append(jnp.where(row8 < k // (i + 1), sv0[i:i + 1] + sv1[0:SUBLANES], NEG_INF))
        cidx.append(si0[i:i + 1] * PEER_NKEYS + si1[0:SUBLANES])
    cand.append(sv0[SUBLANES:k] + sv1[0:1])
    cidx.append(si0[SUBLANES:k] * PEER_NKEYS + si1[0:1])
    cand = jnp.concatenate(cand, axis=0)
    cidx = jnp.concatenate(cidx, axis=0)
    top_s, eidx = _topk_rows(cand, cand.shape[0], payload=cidx)
    e = jnp.exp(top_s - jnp.max(top_s, axis=0, keepdims=True))
    idx_ref[...] = eidx
    gate_ref[...] = e / jnp.sum(e, axis=0, keepdims=True)


def _peer_route(qry, sub_keys, tm):
    t = qry.shape[0]
    tm = min(tm, t)
    nhp = 2 * PEER_HEADS
    half = pl.BlockSpec((None, PEER_TOPK, tm), lambda i, j: (j, 0, i))
    sv, si = pl.pallas_call(
        _peer_subkey_topk_kernel,
        grid=(t // tm, nhp),
        in_specs=[pl.BlockSpec((tm, PEER_SUBK), lambda i, j: (i, j)),
                  pl.BlockSpec((None, PEER_NKEYS, PEER_SUBK), lambda i, j: (j, 0, 0))],
        out_specs=[half, half],
        out_shape=[jax.ShapeDtypeStruct((nhp, PEER_TOPK, t), F32),
                   jax.ShapeDtypeStruct((nhp, PEER_TOPK, t), jnp.int32)],
        compiler_params=_params("parallel", "arbitrary"),
        name="peer_subkey_topk",
    )(qry, sub_keys.reshape(nhp, PEER_NKEYS, PEER_SUBK).astype(BF16))
    first = pl.BlockSpec((None, PEER_TOPK, tm), lambda i, j: (2 * j, 0, i))
    second = pl.BlockSpec((None, PEER_TOPK, tm), lambda i, j: (2 * j + 1, 0, i))
    head = pl.BlockSpec((None, PEER_TOPK, tm), lambda i, j: (j, 0, i))
    eidx, gate = pl.pallas_call(
        _peer_expert_topk_kernel,
        grid=(t // tm, PEER_HEADS),
        in_specs=[first, first, second, second],
        out_specs=[head, head],
        out_shape=[jax.ShapeDtypeStruct((PEER_HEADS, PEER_TOPK, t), jnp.int32),
                   jax.ShapeDtypeStruct((PEER_HEADS, PEER_TOPK, t), F32)],
        compiler_params=_params("parallel", "arbitrary"),
        name="peer_expert_topk",
    )(sv, si, sv, si)
    n_sel = PEER_HEADS * PEER_TOPK
    return eidx.reshape(n_sel, t).T, gate.reshape(n_sel, t).T


N_SEL = PEER_HEADS * PEER_TOPK
GATHER_TOKENS = 8
GATHER_AHEAD = 2
GATHER_SLOTS = GATHER_AHEAD + 1
GATHER_PIPE = 2


def _gelu(x):
    return 0.5 * x * (1.0 + lax.erf(x * (2.0 ** -0.5)))


def _peer_gather_kernel(idx_ref, idx_next_ref, idx_ahead_ref, x_ref, gate_ref, uv_hbm, f_ref,
                        buf_ring, sem, p_ref, w_ref):
    tb = x_ref.shape[0]
    rows = x_ref.shape[1]
    i = pl.program_id(0)
    slot = i % GATHER_SLOTS

    def start_row(ids_ref, dst_slot, t, k):
        pltpu.make_async_copy(uv_hbm.at[ids_ref[t, k]], buf_ring.at[dst_slot, k, t],
                              sem.at[dst_slot]).start(priority=k % 2)

    def wait_slot(s):
        pltpu.make_async_copy(buf_ring.at[s], buf_ring.at[s], sem.at[s]).wait()

    @pl.when(i == 0)
    def _():
        for b, ids_ref in enumerate((idx_ref, idx_next_ref)):
            def per_token(t, carry, b=b, ids_ref=ids_ref):
                for k in range(N_SEL):
                    start_row(ids_ref, b, t, k)
                return carry
            lax.fori_loop(0, tb, per_token, 0)

    half = N_SEL // 2

    def evaluate(buf, ahead_slot):
        def per_token(t, carry):
            x = x_ref[t].astype(F32)
            for k in range(N_SEL):
                if k % 2 == 0:
                    start_row(idx_ahead_ref, ahead_slot, t, k // 2)
                p = buf[k, t, 0:rows, :].astype(F32) * x
                part = p[0:SUBLANES]
                for r in range(1, rows // SUBLANES):
                    part = part + p[r * SUBLANES:(r + 1) * SUBLANES]
                p_ref[pl.ds(k * SUBLANES, SUBLANES), :] = part
            s = p_ref[pl.ds(0, N_SEL, stride=SUBLANES), :]
            for r in range(1, SUBLANES):
                s = s + p_ref[pl.ds(r, N_SEL, stride=SUBLANES), :]
            act = jnp.sum(s, axis=-1, keepdims=True)
            gate_col = jnp.broadcast_to(gate_ref[pl.ds(t, 1), :], (N_SEL, N_SEL)).T
            w_ref[...] = gate_col * _gelu(act)
            acc = jnp.zeros((rows, LANES), F32)
            for k in range(N_SEL):
                if k % 2 == 0:
                    start_row(idx_ahead_ref, ahead_slot, t, half + k // 2)
                acc = acc + w_ref[pl.ds(k, 1), :] * buf[k, t, rows:2 * rows, :].astype(F32)
            f_ref[t] = acc
            return carry

        lax.fori_loop(0, tb, per_token, 0)

    wait_slot(slot)
    evaluate(buf_ring.at[slot], (i + GATHER_AHEAD) % GATHER_SLOTS)

    @pl.when(i == pl.num_programs(0) - 1)
    def _():
        for a in range(1, GATHER_AHEAD + 1):
            wait_slot((i + a) % GATHER_SLOTS)


def _peer_experts(x_bf, eidx, gate, uv):
    t, d = x_bf.shape
    rows = d // LANES
    tb = GATHER_TOKENS
    assert t % tb == 0
    xs = x_bf.reshape(t, rows, LANES)
    nblk = t // tb
    idx_spec = lambda ahead: pl.BlockSpec((tb, N_SEL), lambda i: (jnp.minimum(i + ahead, nblk - 1), 0),
                                          memory_space=pltpu.SMEM)
    f = pl.pallas_call(
        _peer_gather_kernel,
        grid=(nblk,),
        in_specs=[idx_spec(0), idx_spec(1), idx_spec(GATHER_AHEAD),
                  pl.BlockSpec((tb, rows, LANES), lambda i: (i, 0, 0)),
                  pl.BlockSpec((tb, N_SEL), lambda i: (i, 0)),
                  pl.BlockSpec(memory_space=pl.ANY)],
        out_specs=pl.BlockSpec((tb, rows, LANES), lambda i: (i, 0, 0)),
        out_shape=jax.ShapeDtypeStruct((t, rows, LANES), F32),
        scratch_shapes=[pltpu.VMEM((GATHER_SLOTS, N_SEL, tb, 2 * rows, LANES), BF16),
                        pltpu.SemaphoreType.DMA((GATHER_SLOTS,)),
                        pltpu.VMEM((N_SEL * SUBLANES, LANES), F32),
                        pltpu.VMEM((N_SEL, LANES), F32)],
        compiler_params=_params("arbitrary"),
        name="peer_experts",
    )(eidx, eidx, eidx, xs, gate, uv)
    return f.reshape(t, d)


def _peer_ffn(h, h_bf, w_query, sub_keys, u_tab, v_tab, ln_g, ln_b):
    t, d = h.shape
    rows = d // LANES
    n_exp = u_tab.shape[0]
    qry = _matmul(h_bf, w_query, BF16, 1024, 512)
    eidx, gate = _peer_route(qry, sub_keys, 1024)
    uv = jnp.concatenate([u_tab.astype(BF16).reshape(n_exp, rows, LANES),
                          v_tab.astype(BF16).reshape(n_exp, rows, LANES)],
                         axis=1)
    f = _peer_experts(h_bf, eidx, gate, uv)
    return _residual_ln(h, f, ln_g, ln_b, 256)


def _rope_tables(positions):
    half = DIFF_QK_DIM // 2
    inv = jnp.power(ROPE_THETA, -jnp.arange(half, dtype=F32) * (2.0 / DIFF_QK_DIM))
    ang = positions.astype(F32).reshape(-1, 1) * inv
    cos, sin = jnp.cos(ang), jnp.sin(ang)
    return (jnp.concatenate([cos, cos, cos, cos], axis=-1),
            jnp.concatenate([-sin, sin, -sin, sin], axis=-1))


def kernel(x, mem, positions, attn_w_in, attn_lambda, attn_head_norm, mlstm_w_in, mlstm_gate_bias, mlstm_head_norm, mem_w_kv, w_out, ln_mix_g, ln_mix_b, peer_w_query, peer_sub_keys, peer_u, peer_v, ln_ffn_g, ln_ffn_b):
    batch, seq, d = x.shape
    n_mem = mem.shape[1]
    t = batch * seq
    mem_kv = _matmul(mem.reshape(batch * n_mem, d).astype(BF16), mem_w_kv, BF16, 1024, 512)
    cos_t, sin_t = _rope_tables(positions)
    h = x.reshape(t, d)
    h_bf = h.astype(BF16)
    for i in range(DEPTH):
        j = i // 2
        if i % 2 == 0:
            proj = _matmul(h_bf, attn_w_in[j], BF16, 1024, 512)
            qk = _rope(proj, cos_t, sin_t, 512)
            lam_init = 0.8 - 0.6 * math.exp(-0.3 * i)
            mix = _diff_attention(qk, proj, attn_lambda[j], attn_head_norm[j], batch, seq, lam_init, 512)
            qm_block0 = 3 * MIX_HEADS
        else:
            n_main = 4 * MIX_WIDTH + MEM_WIDTH
            w_in = mlstm_w_in[j]
            proj = _matmul(h_bf, w_in, BF16, 1024, 512, n=n_main)
            n_gates = 4 * MIX_HEADS
            w_gates = jnp.pad(w_in[:, n_main:], ((0, 0), (0, LANES - n_gates)))
            gates = _matmul(h_bf, w_gates, F32, 1024, LANES)[:, :n_gates]
            gates_t = gates.reshape(batch, seq // MLSTM_CHUNK, MLSTM_CHUNK, 4, MIX_HEADS).transpose(0, 3, 4, 1, 2)
            mix = _mlstm(proj, gates_t, mlstm_gate_bias[j], mlstm_head_norm[j], batch, seq)
            qm_block0 = 4 * MIX_HEADS
        mem_out = _mem_attention(proj, qm_block0, mem_kv, batch, seq, n_mem, 512)
        h, h_bf = _outproj_ln(mix, mem_out, h, w_out[i], ln_mix_g[i], ln_mix_b[i], 256)
        h, h_bf = _peer_ffn(h, h_bf, peer_w_query[i], peer_sub_keys[i], peer_u[i], peer_v[i],
                            ln_ffn_g[i], ln_ffn_b[i])
    return h.reshape(batch, seq, d)
```

```python
import functools
import math

import jax
import jax.numpy as jnp
from jax import lax
from jax.experimental import pallas as pl
from jax.experimental.pallas import tpu as pltpu

F32 = jnp.float32
BF16 = jnp.bfloat16

HEAD_DIM = 128
MIX_HEADS = 12
MIX_WIDTH = MIX_HEADS * HEAD_DIM
MEM_HEADS = 4
MEM_WIDTH = MEM_HEADS * HEAD_DIM
DIFF_QK_DIM = HEAD_DIM // 2
ROPE_THETA = 10000.0
MLSTM_CHUNK = 128
PEER_HEADS = 8
PEER_NKEYS = 128
PEER_TOPK = 16
PEER_SUBK = 128
DEPTH = 2
DN_ALPHA = (2.0 * DEPTH) ** 0.25
LN_EPS = 1e-5

LANES = 128
SUBLANES = 8
VMEM_LIMIT_BYTES = 48 * 1024 * 1024
NEG_INF = float("-inf")
LOG2_E = math.log2(math.e)
ATTN_SUB_ROWS = 128
MLSTM_UNROLL = 4


def _params(*sem):
    return pltpu.CompilerParams(dimension_semantics=sem, vmem_limit_bytes=VMEM_LIMIT_BYTES)


def _nt_dot(a, b):
    return lax.dot_general(a, b, (((1,), (1,)), ((), ())), preferred_element_type=F32)


def _matmul_kernel(a_ref, b_ref, o_ref, b_bf_ref):
    @pl.when(pl.program_id(1) == 0)
    def _():
        b_bf_ref[...] = b_ref[...].astype(BF16)

    o_ref[...] = jnp.dot(a_ref[...], b_bf_ref[...], preferred_element_type=F32).astype(o_ref.dtype)


def _matmul(a, b, out_dtype, tm, tn, n=None):
    m, k = a.shape
    n = b.shape[1] if n is None else n
    tm, tn = min(tm, m), min(tn, n)
    assert m % tm == 0 and n % tn == 0
    return pl.pallas_call(
        _matmul_kernel,
        grid=(n // tn, m // tm),
        in_specs=[pl.BlockSpec((tm, k), lambda j, i: (i, 0)),
                  pl.BlockSpec((k, tn), lambda j, i: (0, j))],
        out_specs=pl.BlockSpec((tm, tn), lambda j, i: (i, j)),
        out_shape=jax.ShapeDtypeStruct((m, n), out_dtype),
        scratch_shapes=[pltpu.VMEM((k, tn), BF16)],
        compiler_params=_params("arbitrary", "arbitrary"),
        name="matmul",
    )(a, b)


def _matmul_rope_kernel(a_ref, b_ref, cos_ref, sin_ref, o_ref, b_bf_ref, *, q_tiles):
    j = pl.program_id(0)

    @pl.when(pl.program_id(1) == 0)
    def _():
        b_bf_ref[...] = b_ref[...].astype(BF16)

    acc = jnp.dot(a_ref[...], b_bf_ref[...], preferred_element_type=F32)

    @pl.when(j < 2 * q_tiles)
    def _():
        cos, sin = cos_ref[...], sin_ref[...]
        lane = lax.broadcasted_iota(jnp.int32, cos.shape, 1)
        first = (lane % DIFF_QK_DIM) < (DIFF_QK_DIM // 2)
        scale = jnp.where(j < q_tiles, DIFF_QK_DIM ** -0.5 * LOG2_E, 1.0).astype(F32)
        for h in range(acc.shape[1] // HEAD_DIM):
            cols = slice(h * HEAD_DIM, (h + 1) * HEAD_DIM)
            x = acc[:, cols]
            rot = jnp.where(first, pltpu.roll(x, LANES - DIFF_QK_DIM // 2, 1), pltpu.roll(x, DIFF_QK_DIM // 2, 1))
            o_ref[:, cols] = ((x * cos + rot * sin) * scale).astype(o_ref.dtype)

    @pl.when(j >= 2 * q_tiles)
    def _():
        o_ref[...] = acc.astype(o_ref.dtype)


def _matmul_rope(a, b, cos_t, sin_t, tm, tn):
    m, k = a.shape
    n = b.shape[1]
    tm = min(tm, m)
    assert m % tm == 0 and n % tn == 0 and MIX_WIDTH % tn == 0
    tab = pl.BlockSpec((tm, LANES), lambda j, i: (i, 0))
    return pl.pallas_call(
        functools.partial(_matmul_rope_kernel, q_tiles=MIX_WIDTH // tn),
        grid=(n // tn, m // tm),
        in_specs=[pl.BlockSpec((tm, k), lambda j, i: (i, 0)),
                  pl.BlockSpec((k, tn), lambda j, i: (0, j)), tab, tab],
        out_specs=pl.BlockSpec((tm, tn), lambda j, i: (i, j)),
        out_shape=jax.ShapeDtypeStruct((m, n), BF16),
        scratch_shapes=[pltpu.VMEM((k, tn), BF16)],
        compiler_params=_params("arbitrary", "arbitrary"),
        name="matmul_rope",
    )(a, b, cos_t, sin_t)


def _diff_attn_kernel(lam_ref, q_ref, k_ref, v_ref, g_ref, o_ref, vext_ref, *, lam_init):
    @pl.when(pl.program_id(2) == 0)
    def _():
        vext_ref[:, 0:HEAD_DIM] = v_ref[...]
        vext_ref[:, HEAD_DIM:2 * HEAD_DIM] = jnp.ones(v_ref.shape, v_ref.dtype)

    lp = lam_ref[...]
    lam = (jnp.exp(jnp.sum(lp[0:1] * lp[1:2], axis=-1, keepdims=True))
           - jnp.exp(jnp.sum(lp[2:3] * lp[3:4], axis=-1, keepdims=True)) + lam_init)
    sub = ATTN_SUB_ROWS
    for r0 in range(0, q_ref.shape[0], sub):
        q = q_ref[r0:r0 + sub, :]
        lane = lax.broadcasted_iota(jnp.int32, q.shape, 1)
        zero = jnp.zeros_like(q)
        qq = jnp.concatenate([jnp.where(lane < DIFF_QK_DIM, q, zero), jnp.where(lane >= DIFF_QK_DIM, q, zero)],
                             axis=0)
        s = _nt_dot(qq, k_ref[...])
        e = jnp.exp2(s - jnp.max(s, axis=-1, keepdims=True)).astype(BF16)
        pv = jnp.dot(e, vext_ref[...], preferred_element_type=F32)
        pv = pv[:, 0:HEAD_DIM] / pv[:, HEAD_DIM:HEAD_DIM + 1]
        out = pv[:sub] - lam * pv[sub:]
        out = out * lax.rsqrt(jnp.mean(out * out, axis=-1, keepdims=True) + LN_EPS) * g_ref[...] * (1.0 - lam_init)
        o_ref[r0:r0 + sub, :] = out.astype(o_ref.dtype)


def _diff_attention(qk, proj, lam_params, norm_g, batch, seq, lam_init, tq):
    tq = min(tq, seq)
    nq = seq // tq
    return pl.pallas_call(
        functools.partial(_diff_attn_kernel, lam_init=lam_init),
        grid=(batch, MIX_HEADS, nq),
        in_specs=[pl.BlockSpec((4, DIFF_QK_DIM), lambda b, h, i: (0, 0)),
                  pl.BlockSpec((tq, HEAD_DIM), lambda b, h, i: (b * nq + i, h)),
                  pl.BlockSpec((seq, HEAD_DIM), lambda b, h, i: (b, MIX_HEADS + h)),
                  pl.BlockSpec((seq, HEAD_DIM), lambda b, h, i: (b, 2 * MIX_HEADS + h)),
                  pl.BlockSpec((None, 1, HEAD_DIM), lambda b, h, i: (h, 0, 0))],
        out_specs=pl.BlockSpec((tq, HEAD_DIM), lambda b, h, i: (b * nq + i, h)),
        out_shape=jax.ShapeDtypeStruct((batch * seq, MIX_WIDTH), BF16),
        scratch_shapes=[pltpu.VMEM((seq, 2 * HEAD_DIM), BF16)],
        compiler_params=_params("arbitrary", "arbitrary", "arbitrary"),
        name="diff_attention",
    )(lam_params, qk, qk, proj, norm_g.reshape(MIX_HEADS, 1, HEAD_DIM))


def _mem_attn_kernel(q_ref, k_ref, v_ref, o_ref):
    s = _nt_dot(q_ref[...], k_ref[...]) * (HEAD_DIM ** -0.5)
    e = jnp.exp(s - jnp.max(s, axis=-1, keepdims=True))
    l = jnp.sum(e, axis=-1, keepdims=True)
    o_ref[...] = (jnp.dot(e.astype(BF16), v_ref[...], preferred_element_type=F32) / l).astype(o_ref.dtype)


def _mem_attention(proj, qm_block0, mem_kv, batch, seq, n_mem, tq):
    tq = min(tq, seq)
    nq = seq // tq
    return pl.pallas_call(
        _mem_attn_kernel,
        grid=(batch, MEM_HEADS, nq),
        in_specs=[pl.BlockSpec((tq, HEAD_DIM), lambda b, h, i: (b * nq + i, qm_block0 + h)),
                  pl.BlockSpec((n_mem, HEAD_DIM), lambda b, h, i: (b, h)),
                  pl.BlockSpec((n_mem, HEAD_DIM), lambda b, h, i: (b, MEM_HEADS + h))],
        out_specs=pl.BlockSpec((tq, HEAD_DIM), lambda b, h, i: (b * nq + i, h)),
        out_shape=jax.ShapeDtypeStruct((batch * seq, MEM_WIDTH), BF16),
        compiler_params=_params("parallel", "arbitrary", "arbitrary"),
        name="mem_attention",
    )(proj, mem_kv, mem_kv)


def _layer_norm_rows(z, g, b):
    mu = jnp.mean(z, axis=-1, keepdims=True)
    zc = z - mu
    var = jnp.mean(zc * zc, axis=-1, keepdims=True)
    return zc * lax.rsqrt(var + LN_EPS) * g + b


def _outproj_ln_kernel(mix_ref, mem_ref, h_ref, w_mix_ref, w_mem_ref, g_ref, b_ref, o_ref, obf_ref):
    y = (jnp.dot(mix_ref[...], w_mix_ref[...], preferred_element_type=F32)
         + jnp.dot(mem_ref[...], w_mem_ref[...], preferred_element_type=F32))
    out = _layer_norm_rows(DN_ALPHA * h_ref[...] + y, g_ref[...], b_ref[...])
    o_ref[...] = out
    obf_ref[...] = out.astype(BF16)


def _outproj_ln(mix, mem_out, h, w_out, g, b, tm):
    t, d = h.shape
    tm = min(tm, t)
    row = lambda i: (i, 0)
    fixed = lambda i: (0, 0)
    return pl.pallas_call(
        _outproj_ln_kernel,
        grid=(t // tm,),
        in_specs=[pl.BlockSpec((tm, MIX_WIDTH), row), pl.BlockSpec((tm, MEM_WIDTH), row),
                  pl.BlockSpec((tm, d), row),
                  pl.BlockSpec((MIX_WIDTH, d), fixed), pl.BlockSpec((MEM_WIDTH, d), fixed),
                  pl.BlockSpec((1, d), fixed), pl.BlockSpec((1, d), fixed)],
        out_specs=[pl.BlockSpec((tm, d), row), pl.BlockSpec((tm, d), row)],
        out_shape=[jax.ShapeDtypeStruct((t, d), F32), jax.ShapeDtypeStruct((t, d), BF16)],
        compiler_params=_params("parallel"),
        name="outproj_ln",
    )(mix, mem_out, h, w_out[:MIX_WIDTH].astype(BF16), w_out[MIX_WIDTH:].astype(BF16),
      g.reshape(1, d), b.reshape(1, d))


def _residual_ln_kernel(h_ref, f_ref, g_ref, b_ref, o_ref, obf_ref):
    out = _layer_norm_rows(DN_ALPHA * h_ref[...] + f_ref[...], g_ref[...], b_ref[...])
    o_ref[...] = out
    obf_ref[...] = out.astype(BF16)


def _residual_ln(h, f, g, b, tm):
    t, d = h.shape
    tm = min(tm, t)
    row = lambda i: (i, 0)
    fixed = lambda i: (0, 0)
    return pl.pallas_call(
        _residual_ln_kernel,
        grid=(t // tm,),
        in_specs=[pl.BlockSpec((tm, d), row), pl.BlockSpec((tm, d), row),
                  pl.BlockSpec((1, d), fixed), pl.BlockSpec((1, d), fixed)],
        out_specs=[pl.BlockSpec((tm, d), row), pl.BlockSpec((tm, d), row)],
        out_shape=[jax.ShapeDtypeStruct((t, d), F32), jax.ShapeDtypeStruct((t, d), BF16)],
        compiler_params=_params("parallel"),
        name="residual_ln",
    )(h, f, g.reshape(1, d), b.reshape(1, d))


def _lane_cumsum(x, reverse):
    lane = lax.broadcasted_iota(jnp.int32, x.shape, 1)
    sh = 1
    while sh < LANES:
        if reverse:
            x = x + jnp.where(lane < LANES - sh, pltpu.roll(x, LANES - sh, 1), 0.0)
        else:
            x = x + jnp.where(lane >= sh, pltpu.roll(x, sh, 1), 0.0)
        sh *= 2
    return x


def _log_sigmoid(x):
    return jnp.minimum(x, 0.0) - jnp.log1p(jnp.exp(-jnp.abs(x)))


def _mlstm_kernel(bias_ref, q_ref, k_ref, v_ref, o_ref, g_ref, ng_ref, out_ref,
                  brow_ref, ig_ref, blast_ref, amax_ref, ms_ref, inner_ref, mi_ref, bcol_ref, kv_ref, cs_ref,
                  state_ref, m_ref):
    L = MLSTM_CHUNK
    D = HEAD_DIM
    scale = HEAD_DIM ** -0.5
    hd = pl.program_id(1)
    nc = q_ref.shape[0] // L
    jj = lax.broadcasted_iota(jnp.int32, (L, L), 0)
    ss = lax.broadcasted_iota(jnp.int32, (L, L), 1)
    masks = (ss <= jj, ss >= jj)
    ones_blk = jnp.ones((L, D), BF16)

    for d in range(2):
        lf = _log_sigmoid(g_ref[2 * d + 1] + bias_ref[2 * d + 1, hd])
        brow_ref[d] = _lane_cumsum(lf, reverse=(d == 1))
        blast_ref[d] = jnp.broadcast_to(jnp.sum(lf, axis=-1, keepdims=True), (nc, L))
        ig_ref[d] = g_ref[2 * d] + bias_ref[2 * d, hd]

    def pass1(c, carry):
        r0 = pl.multiple_of(c * L, L)
        rows = pl.ds(r0, L)
        q = q_ref[rows, :]
        k = k_ref[rows, :]
        v = v_ref[rows, :]
        kt = k.astype(F32).T.astype(BF16)
        qk = _nt_dot(q, k) * scale
        for d in range(2):
            b_row = jnp.broadcast_to(brow_ref[d, pl.ds(c, 1), :], (L, L))
            ig_row = jnp.broadcast_to(ig_ref[d, pl.ds(c, 1), :], (L, L))
            b_col = b_row.T
            ig_col = ig_row.T
            dm = jnp.where(masks[d], b_col - b_row + ig_row, NEG_INF)
            m_intra = jnp.max(dm, axis=-1, keepdims=True)
            s = (qk * jnp.exp(dm - m_intra)).astype(BF16)
            inner_ref[d, rows, :] = jnp.dot(s, jnp.concatenate([v, ones_blk], axis=1), preferred_element_type=F32)
            mi_ref[d, rows, :] = jnp.broadcast_to(m_intra, (L, D))
            bcol_ref[d, rows, :] = b_col
            a = blast_ref[d, pl.ds(c, 1), :] - b_col + ig_col
            a_max = jnp.max(a, axis=0, keepdims=True)
            wa = jnp.exp(a - a_max)
            vw = jnp.concatenate([v.astype(F32) * wa, wa], axis=1).astype(BF16)
            kv_ref[d, c] = jnp.dot(kt, vw, preferred_element_type=F32) * scale
            amax_ref[d, pl.ds(c, 1), :] = a_max
        return carry

    unroll = math.gcd(nc, MLSTM_UNROLL)

    def unrolled(body):
        def group(g, carry):
            for u in range(unroll):
                carry = body(g * unroll + u, carry)
            return carry
        return group

    lax.fori_loop(0, nc // unroll, unrolled(pass1), 0)

    state_ref[...] = jnp.zeros_like(state_ref)
    m_ref[...] = jnp.zeros_like(m_ref)

    def pass2(step, carry):
        for d in range(2):
            c = step if d == 0 else nc - 1 - step
            st = state_ref[d]
            m_old = m_ref[d]
            cs_ref[d, c] = st.astype(BF16)
            ms_ref[d, pl.ds(c, 1), :] = m_old
            b_last = blast_ref[d, pl.ds(c, 1), :]
            a_max = amax_ref[d, pl.ds(c, 1), :]
            m_new = jnp.maximum(b_last + m_old, a_max)
            decay = jnp.exp(b_last + m_old - m_new)
            w = jnp.exp(a_max - m_new)
            wide = lambda x: jnp.concatenate([x, x], axis=1)
            state_ref[d] = wide(decay) * st + wide(w) * kv_ref[d, c]
            m_ref[d] = m_new
        return carry

    lax.fori_loop(0, nc, pass2, 0)

    def pass3(c, carry):
        r0 = pl.multiple_of(c * L, L)
        rows = pl.ds(r0, L)
        q = q_ref[rows, :]
        hs = jnp.zeros((L, D), F32)
        for d in range(2):
            qc = jnp.dot(q, cs_ref[d, c], preferred_element_type=F32)
            inter = bcol_ref[d, rows, :] + ms_ref[d, pl.ds(c, 1), :]
            mi = mi_ref[d, rows, :]
            m_j = jnp.maximum(inter, mi)
            fi = jnp.exp(mi - m_j)
            wi = jnp.exp(inter - m_j)
            inn = inner_ref[d, rows, :]
            num = wi * qc[:, 0:D] + fi * inn[:, 0:D]
            den = wi * qc[:, D:2 * D] + fi * inn[:, D:2 * D]
            hs = hs + num / jnp.maximum(jnp.abs(den), jnp.exp(-m_j))
        hs = hs * lax.rsqrt(jnp.mean(hs * hs, axis=-1, keepdims=True) + LN_EPS) * ng_ref[...]
        out_ref[rows, :] = (jax.nn.sigmoid(o_ref[rows, :].astype(F32)) * hs).astype(out_ref.dtype)
        return carry

    lax.fori_loop(0, nc // unroll, unrolled(pass3), 0)


def _mlstm(proj, gates_t, gate_bias, norm_g, batch, seq):
    L, D = MLSTM_CHUNK, HEAD_DIM
    nc = seq // L
    col = lambda off: pl.BlockSpec((seq, D), lambda b, h: (b, off + h))
    st = lambda shape, dtype=F32: pltpu.VMEM(shape, dtype)
    return pl.pallas_call(
        _mlstm_kernel,
        grid=(batch, MIX_HEADS),
        in_specs=[pl.BlockSpec(memory_space=pltpu.SMEM),
                  col(0), col(MIX_HEADS), col(2 * MIX_HEADS), col(3 * MIX_HEADS),
                  pl.BlockSpec((None, 4, None, nc, L), lambda b, h: (b, 0, h, 0, 0)),
                  pl.BlockSpec((None, 1, D), lambda b, h: (h, 0, 0))],
        out_specs=pl.BlockSpec((seq, D), lambda b, h: (b, h)),
        out_shape=jax.ShapeDtypeStruct((batch * seq, MIX_WIDTH), BF16),
        scratch_shapes=[st((2, nc, L)), st((2, nc, L)), st((2, nc, L)), st((2, nc, L)), st((2, nc, L)),
                        st((2, seq, 2 * D)), st((2, seq, D)), st((2, seq, D)),
                        st((2, nc, D, 2 * D)), st((2, nc, D, 2 * D), BF16),
                        st((2, D, 2 * D)), st((2, 1, L))],
        compiler_params=_params("parallel", "arbitrary"),
        name="mlstm",
    )(gate_bias, proj, proj, proj, proj, gates_t, norm_g.reshape(MIX_HEADS, 1, HEAD_DIM))


def _topk_rows(vals, n_rows, payload=None):
    tm = vals.shape[1]
    row = lax.broadcasted_iota(jnp.int32, vals.shape, 0)
    row8 = lax.broadcasted_iota(jnp.int32, (SUBLANES, tm), 0)

    def better(a, b):
        keep = a[0] >= b[0]
        return tuple(jnp.where(keep, x, y) for x, y in zip(a, b))

    top_v, top_i = [], []
    for _ in range(PEER_TOPK):
        items = []
        for g in range(n_rows // SUBLANES):
            sl = slice(g * SUBLANES, (g + 1) * SUBLANES)
            items.append((vals[sl], row8 + g * SUBLANES) + (() if payload is None else (payload[sl],)))
        while len(items) > 1:
            nxt = [better(items[j], items[j + 1]) for j in range(0, len(items) - 1, 2)]
            items = nxt + ([items[-1]] if len(items) % 2 else [])
        v8, r8 = items[0][0], items[0][1]
        mx = jnp.max(v8, axis=0, keepdims=True)
        pos = jnp.min(jnp.where(v8 == mx, r8, n_rows), axis=0, keepdims=True)
        top_v.append(mx)
        top_i.append(pos if payload is None else jnp.max(jnp.where(r8 == pos, items[0][2], -1), axis=0, keepdims=True))
        vals = jnp.where(row == pos, NEG_INF, vals)
    return jnp.concatenate(top_v, axis=0), jnp.concatenate(top_i, axis=0)


def _peer_subkey_topk_kernel(q_ref, sk_ref, sv_ref, si_ref):
    s_t = _nt_dot(sk_ref[...], q_ref[...])
    sv, si = _topk_rows(s_t, PEER_NKEYS)
    sv_ref[...] = sv
    si_ref[...] = si


def _peer_expert_topk_kernel(sv0_ref, si0_ref, sv1_ref, si1_ref, idx_ref, gate_ref):
    sv0, si0, sv1, si1 = sv0_ref[...], si0_ref[...], sv1_ref[...], si1_ref[...]
    k = PEER_TOPK
    row8 = lax.broadcasted_iota(jnp.int32, (SUBLANES, sv0.shape[1]), 0)
    cand = [sv0[0:1] + sv1]
    cidx = [si0[0:1] * PEER_NKEYS + si1]
    for i in range(1, SUBLANES):
        cand.append(jnp.where(row8 < k // (i + 1), sv0[i:i + 1] + sv1[0:SUBLANES], NEG_INF))
        cidx.append(si0[i:i + 1] * PEER_NKEYS + si1[0:SUBLANES])
    cand.append(sv0[SUBLANES:k] + sv1[0:1])
    cidx.append(si0[SUBLANES:k] * PEER_NKEYS + si1[0:1])
    cand = jnp.concatenate(cand, axis=0)
    cidx = jnp.concatenate(cidx, axis=0)
    top_s, eidx = _topk_rows(cand, cand.shape[0], payload=cidx)
    e = jnp.exp(top_s - jnp.max(top_s, axis=0, keepdims=True))
    idx_ref[...] = eidx
    gate_ref[...] = e / jnp.sum(e, axis=0, keepdims=True)


def _peer_route(qry, sub_keys, tm):
    t = qry.shape[0]
    tm = min(tm, t)
    nhp = 2 * PEER_HEADS
    half = pl.BlockSpec((None, PEER_TOPK, tm), lambda i, j: (j, 0, i))
    sv, si = pl.pallas_call(
        _peer_subkey_topk_kernel,
        grid=(t // tm, nhp),
        in_specs=[pl.BlockSpec((tm, PEER_SUBK), lambda i, j: (i, j)),
                  pl.BlockSpec((None, PEER_NKEYS, PEER_SUBK), lambda i, j: (j, 0, 0))],
        out_specs=[half, half],
        out_shape=[jax.ShapeDtypeStruct((nhp, PEER_TOPK, t), F32),
                   jax.ShapeDtypeStruct((nhp, PEER_TOPK, t), jnp.int32)],
        compiler_params=_params("parallel", "arbitrary"),
        name="peer_subkey_topk",
    )(qry, sub_keys.reshape(nhp, PEER_NKEYS, PEER_SUBK).astype(BF16))
    first = pl.BlockSpec((None, PEER_TOPK, tm), lambda i, j: (2 * j, 0, i))
    second = pl.BlockSpec((None, PEER_TOPK, tm), lambda i, j: (2 * j + 1, 0, i))
    head = pl.BlockSpec((None, PEER_TOPK, tm), lambda i, j: (j, 0, i))
    eidx, gate = pl.pallas_call(
        _peer_expert_topk_kernel,
        grid=(t // tm, PEER_HEADS),
        in_specs=[first, first, second, second],
        out_specs=[head, head],
        out_shape=[jax.ShapeDtypeStruct((PEER_HEADS, PEER_TOPK, t), jnp.int32),
                   jax.ShapeDtypeStruct((PEER_HEADS, PEER_TOPK, t), F32)],
        compiler_params=_params("parallel", "arbitrary"),
        name="peer_expert_topk",
    )(sv, si, sv, si)
    n_sel = PEER_HEADS * PEER_TOPK
    return eidx.reshape(n_sel, t).T, gate.reshape(n_sel, t).T


N_SEL = PEER_HEADS * PEER_TOPK
GATHER_TOKENS = 8
GATHER_AHEAD = 2
GATHER_SLOTS = GATHER_AHEAD + 1
GATHER_PIPE = 2


def _gelu(x):
    return 0.5 * x * (1.0 + lax.erf(x * (2.0 ** -0.5)))


def _peer_gather_kernel(idx_ref, idx_next_ref, idx_ahead_ref, x_ref, gate_ref, uv_hbm, f_ref,
                        buf_ring, sem, p_ref, w_ref):
    tb = x_ref.shape[0]
    rows = x_ref.shape[1]
    i = pl.program_id(0)
    slot = i % GATHER_SLOTS

    def start_row(ids_ref, dst_slot, t, k):
        pltpu.make_async_copy(uv_hbm.at[ids_ref[t, k]], buf_ring.at[dst_slot, k, t],
                              sem.at[dst_slot]).start(priority=k % 2)

    def wait_slot(s):
        pltpu.make_async_copy(buf_ring.at[s], buf_ring.at[s], sem.at[s]).wait()

    @pl.when(i == 0)
    def _():
        for b, ids_ref in enumerate((idx_ref, idx_next_ref)):
            def per_token(t, carry, b=b, ids_ref=ids_ref):
                for k in range(N_SEL):
                    start_row(ids_ref, b, t, k)
                return carry
            lax.fori_loop(0, tb, per_token, 0)

    half = N_SEL // 2

    def evaluate(buf, ahead_slot):
        def per_token(t, carry):
            x = x_ref[t].astype(F32)
            for k in range(N_SEL):
                if k % 2 == 0:
                    start_row(idx_ahead_ref, ahead_slot, t, k // 2)
                p = buf[k, t, 0:rows, :].astype(F32) * x
                part = p[0:SUBLANES]
                for r in range(1, rows // SUBLANES):
                    part = part + p[r * SUBLANES:(r + 1) * SUBLANES]
                p_ref[pl.ds(k * SUBLANES, SUBLANES), :] = part
            s = p_ref[pl.ds(0, N_SEL, stride=SUBLANES), :]
            for r in range(1, SUBLANES):
                s = s + p_ref[pl.ds(r, N_SEL, stride=SUBLANES), :]
            act = jnp.sum(s, axis=-1, keepdims=True)
            gate_col = jnp.broadcast_to(gate_ref[pl.ds(t, 1), :], (N_SEL, N_SEL)).T
            w_ref[...] = gate_col * _gelu(act)
            acc = jnp.zeros((rows, LANES), F32)
            for k in range(N_SEL):
                if k % 2 == 0:
                    start_row(idx_ahead_ref, ahead_slot, t, half + k // 2)
                acc = acc + w_ref[pl.ds(k, 1), :] * buf[k, t, rows:2 * rows, :].astype(F32)
            f_ref[t] = acc
            return carry

        lax.fori_loop(0, tb, per_token, 0)

    wait_slot(slot)
    evaluate(buf_ring.at[slot], (i + GATHER_AHEAD) % GATHER_SLOTS)

    @pl.when(i == pl.num_programs(0) - 1)
    def _():
        for a in range(1, GATHER_AHEAD + 1):
            wait_slot((i + a) % GATHER_SLOTS)


def _peer_experts(x_bf, eidx, gate, uv):
    t, d = x_bf.shape
    rows = d // LANES
    tb = GATHER_TOKENS
    assert t % tb == 0
    xs = x_bf.reshape(t, rows, LANES)
    nblk = t // tb
    idx_spec = lambda ahead: pl.BlockSpec((tb, N_SEL), lambda i: (jnp.minimum(i + ahead, nblk - 1), 0),
                                          memory_space=pltpu.SMEM)
    f = pl.pallas_call(
        _peer_gather_kernel,
        grid=(nblk,),
        in_specs=[idx_spec(0), idx_spec(1), idx_spec(GATHER_AHEAD),
                  pl.BlockSpec((tb, rows, LANES), lambda i: (i, 0, 0)),
                  pl.BlockSpec((tb, N_SEL), lambda i: (i, 0)),
                  pl.BlockSpec(memory_space=pl.ANY)],
        out_specs=pl.BlockSpec((tb, rows, LANES), lambda i: (i, 0, 0)),
        out_shape=jax.ShapeDtypeStruct((t, rows, LANES), F32),
        scratch_shapes=[pltpu.VMEM((GATHER_SLOTS, N_SEL, tb, 2 * rows, LANES), BF16),
                        pltpu.SemaphoreType.DMA((GATHER_SLOTS,)),
                        pltpu.VMEM((N_SEL * SUBLANES, LANES), F32),
                        pltpu.VMEM((N_SEL, LANES), F32)],
        compiler_params=_params("arbitrary"),
        name="peer_experts",
    )(eidx, eidx, eidx, xs, gate, uv)
    return f.reshape(t, d)


def _peer_ffn(h, h_bf, w_query, sub_keys, u_tab, v_tab, ln_g, ln_b):
    t, d = h.shape
    rows = d // LANES
    n_exp = u_tab.shape[0]
    qry = _matmul(h_bf, w_query, BF16, 1024, 512)
    eidx, gate = _peer_route(qry, sub_keys, 1024)
    uv = jnp.concatenate([u_tab.astype(BF16).reshape(n_exp, rows, LANES),
                          v_tab.astype(BF16).reshape(n_exp, rows, LANES)],
                         axis=1)
    f = _peer_experts(h_bf, eidx, gate, uv)
    return _residual_ln(h, f, ln_g, ln_b, 256)


def _rope_tables(positions):
    half = DIFF_QK_DIM // 2
    inv = jnp.power(ROPE_THETA, -jnp.arange(half, dtype=F32) * (2.0 / DIFF_QK_DIM))
    ang = positions.astype(F32).reshape(-1, 1) * inv
    cos, sin = jnp.cos(ang), jnp.sin(ang)
    return (jnp.concatenate([cos, cos, cos, cos], axis=-1),
            jnp.concatenate([-sin, sin, -sin, sin], axis=-1))


def kernel(x, mem, positions, attn_w_in, attn_lambda, attn_head_norm, mlstm_w_in, mlstm_gate_bias, mlstm_head_norm, mem_w_kv, w_out, ln_mix_g, ln_mix_b, peer_w_query, peer_sub_keys, peer_u, peer_v, ln_ffn_g, ln_ffn_b):
    batch, seq, d = x.shape
    n_mem = mem.shape[1]
    t = batch * seq
    mem_kv = _matmul(mem.reshape(batch * n_mem, d).astype(BF16), mem_w_kv, BF16, 1024, 512)
    cos_t, sin_t = _rope_tables(positions)
    h = x.reshape(t, d)
    h_bf = h.astype(BF16)
    for i in range(DEPTH):
        j = i // 2
        if i % 2 == 0:
            proj = _matmul_rope(h_bf, attn_w_in[j], cos_t, sin_t, 1024, 512)
            lam_init = 0.8 - 0.6 * math.exp(-0.3 * i)
            mix = _diff_attention(proj, proj, attn_lambda[j], attn_head_norm[j], batch, seq, lam_init, 512)
            qm_block0 = 3 * MIX_HEADS
        else:
            n_main = 4 * MIX_WIDTH + MEM_WIDTH
            w_in = mlstm_w_in[j]
            proj = _matmul(h_bf, w_in, BF16, 1024, 512, n=n_main)
            n_gates = 4 * MIX_HEADS
            w_gates = jnp.pad(w_in[:, n_main:], ((0, 0), (0, LANES - n_gates)))
            gates = _matmul(h_bf, w_gates, F32, 1024, LANES)[:, :n_gates]
            gates_t = gates.reshape(batch, seq // MLSTM_CHUNK, MLSTM_CHUNK, 4, MIX_HEADS).transpose(0, 3, 4, 1, 2)
            mix = _mlstm(proj, gates_t, mlstm_gate_bias[j], mlstm_head_norm[j], batch, seq)
            qm_block0 = 4 * MIX_HEADS
        mem_out = _mem_attention(proj, qm_block0, mem_kv, batch, seq, n_mem, 512)
        h, h_bf = _outproj_ln(mix, mem_out, h, w_out[i], ln_mix_g[i], ln_mix_b[i], 256)
        h, h_bf = _peer_ffn(h, h_bf, peer_w_query[i], peer_sub_keys[i], peer_u[i], peer_v[i],
                            ln_ffn_g[i], ln_ffn_b[i])
    return h.reshape(batch, seq, d)
```
